```python
import math
import jax, jax.numpy as jnp
from jax import lax
import numpy as np

D_MODEL = 2048
BATCH = 2
SEQ = 4096
DEPTH = 1

N_META = 16
Q_BLOCK = 128
MLA_HEADS = 8
MLA_NOPE = 128
MLA_ROPE = 64
MLA_V = 128
Q_LORA = 512
KV_LORA = 256
ROPE_THETA = 10000.0
DIFF_HEADS = 4
DIFF_DK = 128
DIFF_V = 2 * DIFF_DK
DIFF_QK = DIFF_HEADS * 2 * DIFF_DK
D_MIX = MLA_HEADS * MLA_V + DIFF_HEADS * DIFF_V
NUM_BUCKETS = 32
REL_MAX_DIST = 128
N_BIAS_MAPS = 2 * DIFF_HEADS
D_FF = 5632
CONV_W = 3
EPS = 1e-6
IN_SPLITS = (Q_LORA, KV_LORA, MLA_ROPE, DIFF_QK, DIFF_QK, DIFF_HEADS * DIFF_V)
D_IN = sum(IN_SPLITS)

kernel_name = "hymba_mla_diffattn_convffn_layer"


def rmsnorm(x, g):
    xf = x.astype(jnp.float32)
    y = xf * lax.rsqrt(jnp.mean(xf * xf, axis=-1, keepdims=True) + EPS)
    return (y * g.astype(jnp.float32)).astype(x.dtype)


def rope(x, pos):
    half = x.shape[-1] // 2
    inv = ROPE_THETA ** (-jnp.arange(half, dtype=jnp.float32) / half)
    ang = pos.astype(jnp.float32)[:, None] * inv[None, :]
    shp = (1, pos.shape[0]) + (1,) * (x.ndim - 3) + (half,)
    cos, sin = jnp.cos(ang).reshape(shp), jnp.sin(ang).reshape(shp)
    xf = x.astype(jnp.float32)
    x1, x2 = xf[..., :half], xf[..., half:]
    return jnp.concatenate([x1 * cos - x2 * sin, x2 * cos + x1 * sin], axis=-1).astype(x.dtype)


def t5_bucket(rel):
    n = jnp.maximum(rel, 0)
    max_exact = NUM_BUCKETS // 2
    nf = jnp.maximum(n, 1).astype(jnp.float32)
    large = max_exact + (jnp.log(nf / max_exact) / math.log(REL_MAX_DIST / max_exact)
                         * (NUM_BUCKETS - max_exact)).astype(jnp.int32)
    large = jnp.minimum(large, NUM_BUCKETS - 1)
    return jnp.where(n < max_exact, n, large)


def token_mixer(n, pos, rel_bias, w_in, w_uq, w_ukv, w_o, g_cq, g_ckv,
                lq1, lk1, lq2, lk2, g_sub, layer_idx):
    B, Lp, _ = n.shape
    nb = Lp // Q_BLOCK
    proj = n @ w_in
    cuts = list(np.cumsum(IN_SPLITS)[:-1])
    c_q, c_kv, k_r, q_d, k_d, v_d = jnp.split(proj, cuts, axis=-1)

    c_q = rmsnorm(c_q, g_cq)
    c_kv = rmsnorm(c_kv, g_ckv)
    q_m = (c_q @ w_uq).reshape(B, Lp, MLA_HEADS, MLA_NOPE + MLA_ROPE)
    q_nope = q_m[..., :MLA_NOPE]
    q_rope = rope(q_m[..., MLA_NOPE:], pos)
    kv = (c_kv @ w_ukv).reshape(B, Lp, MLA_HEADS, MLA_NOPE + MLA_V)
    k_nope = kv[..., :MLA_NOPE].transpose(0, 2, 1, 3)
    v_m = kv[..., MLA_NOPE:].transpose(0, 2, 1, 3)
    k_rope = rope(k_r, pos)
    mla_scale = 1.0 / math.sqrt(MLA_NOPE + MLA_ROPE)

    q_d = q_d.reshape(B, Lp, DIFF_HEADS, 2, DIFF_DK)
    k_d = k_d.reshape(B, Lp, DIFF_HEADS, 2, DIFF_DK).transpose(0, 2, 3, 1, 4)
    v_d = v_d.reshape(B, Lp, DIFF_HEADS, DIFF_V).transpose(0, 2, 1, 3)
    lam_init = 0.8 - 0.6 * math.exp(-0.3 * layer_idx)
    lam = (jnp.exp(jnp.sum(lq1.astype(jnp.float32) * lk1.astype(jnp.float32)))
           - jnp.exp(jnp.sum(lq2.astype(jnp.float32) * lk2.astype(jnp.float32))) + lam_init)
    diff_scale = 1.0 / math.sqrt(DIFF_DK)

    qn_b = q_nope.reshape(B, nb, Q_BLOCK, MLA_HEADS, MLA_NOPE).transpose(1, 0, 3, 2, 4)
    qr_b = q_rope.reshape(B, nb, Q_BLOCK, MLA_HEADS, MLA_ROPE).transpose(1, 0, 3, 2, 4)
    qd_b = q_d.reshape(B, nb, Q_BLOCK, DIFF_HEADS, 2, DIFF_DK).transpose(1, 0, 3, 4, 2, 5)
    q_start = jnp.arange(nb, dtype=jnp.int32) * Q_BLOCK
    kpos = jnp.arange(Lp, dtype=jnp.int32)
    rel_bias_f = rel_bias.astype(jnp.float32)

    def attend(args):
        qn, qr, qd, q0 = args
        qpos = q0 + jnp.arange(Q_BLOCK, dtype=jnp.int32)
        rel = qpos[:, None] - kpos[None, :]
        causal = rel >= 0
        s_m = (jnp.einsum('bhqd,bhkd->bhqk', qn, k_nope)
               + jnp.einsum('bhqr,bkr->bhqk', qr, k_rope)).astype(jnp.float32) * mla_scale
        p_m = jax.nn.softmax(jnp.where(causal, s_m, -jnp.inf), axis=-1)
        o_m = jnp.einsum('bhqk,bhkd->bqhd', p_m.astype(v_m.dtype), v_m)
        bias = rel_bias_f[t5_bucket(rel)]
        bias = bias.reshape(Q_BLOCK, Lp, DIFF_HEADS, 2).transpose(2, 3, 0, 1)
        s_d = jnp.einsum('bhcqd,bhckd->bhcqk', qd, k_d).astype(jnp.float32) * diff_scale + bias
        p_d = jax.nn.softmax(jnp.where(causal, s_d, -jnp.inf), axis=-1)
        a = p_d[:, :, 0] - lam * p_d[:, :, 1]
        o_d = jnp.einsum('bhqk,bhkd->bqhd', a.astype(v_d.dtype), v_d)
        return o_m, o_d

    o_m, o_d = lax.map(attend, (qn_b, qr_b, qd_b, q_start))
    o_m = o_m.transpose(1, 0, 2, 3, 4).reshape(B, Lp, MLA_HEADS * MLA_V)
    o_d = o_d.transpose(1, 0, 2, 3, 4).reshape(B, Lp, DIFF_HEADS, DIFF_V)
    o_d = (rmsnorm(o_d, g_sub) * (1.0 - lam_init)).reshape(B, Lp, DIFF_HEADS * DIFF_V)
    return jnp.concatenate([o_m, o_d], axis=-1) @ w_o


def conv_gated_mlp(n, w_up, conv_w, conv_b, w_down):
    Lp = n.shape[1]
    up = n @ w_up
    upp = jnp.pad(up, ((0, 0), (CONV_W - 1, 0), (0, 0)))
    conv = conv_b
    for j in range(CONV_W):
        conv = conv + conv_w[j] * upp[:, j:j + Lp]
    gate, val = jnp.split(conv, 2, axis=-1)
    return (jax.nn.silu(gate) * val) @ w_down


def setup_inputs(seed: int = 0) -> dict:
    key = jax.random.key(seed)
    ks = jax.random.split(key, 24)
    f32 = jnp.float32
    nrm = lambda k, shp, s: jax.random.normal(k, shp, f32) * s
    gain = lambda k, shp: 1.0 + 0.05 * jax.random.normal(k, shp, f32)
    L = DEPTH
    return {
        "x": nrm(ks[0], (BATCH, SEQ, D_MODEL), 1.0),
        "meta_tokens": nrm(ks[1], (N_META, D_MODEL), 1.0),
        "rel_bias": nrm(ks[2], (NUM_BUCKETS, N_BIAS_MAPS), 0.5),
        "w_in": nrm(ks[3], (L, D_MODEL, D_IN), D_MODEL ** -0.5),
        "w_uq": nrm(ks[4], (L, Q_LORA, MLA_HEADS * (MLA_NOPE + MLA_ROPE)), Q_LORA ** -0.5),
        "w_ukv": nrm(ks[5], (L, KV_LORA, MLA_HEADS * (MLA_NOPE + MLA_V)), KV_LORA ** -0.5),
        "w_o": nrm(ks[6], (L, D_MIX, D_MODEL), D_MIX ** -0.5),
        "g_attn_pre": gain(ks[7], (L, D_MODEL)),
        "g_attn_post": gain(ks[8], (L, D_MODEL)),
        "g_cq": gain(ks[9], (L, Q_LORA)),
        "g_ckv": gain(ks[10], (L, KV_LORA)),
        "lambda_q1": nrm(ks[11], (L, DIFF_DK), 0.1),
        "lambda_k1": nrm(ks[12], (L, DIFF_DK), 0.1),
        "lambda_q2": nrm(ks[13], (L, DIFF_DK), 0.1),
        "lambda_k2": nrm(ks[14], (L, DIFF_DK), 0.1),
        "g_diff_sub": gain(ks[15], (L, DIFF_V)),
        "g_ffn_pre": gain(ks[16], (L, D_MODEL)),
        "g_ffn_post": gain(ks[17], (L, D_MODEL)),
        "w_up": nrm(ks[18], (L, D_MODEL, 2 * D_FF), D_MODEL ** -0.5),
        "conv_w": nrm(ks[19], (L, CONV_W, 2 * D_FF), CONV_W ** -0.5),
        "conv_b": nrm(ks[20], (L, 2 * D_FF), 0.01),
        "w_down": nrm(ks[21], (L, D_FF, D_MODEL), D_FF ** -0.5),
    }


def reference(x, meta_tokens, rel_bias, w_in, w_uq, w_ukv, w_o, g_attn_pre, g_attn_post,
              g_cq, g_ckv, lambda_q1, lambda_k1, lambda_q2, lambda_k2, g_diff_sub,
              g_ffn_pre, g_ffn_post, w_up, conv_w, conv_b, w_down):
    B, S, D = x.shape
    L = S + N_META
    Lp = ((L + Q_BLOCK - 1) // Q_BLOCK) * Q_BLOCK
    meta = jnp.broadcast_to(meta_tokens[None].astype(x.dtype), (B, N_META, D))
    pad = jnp.zeros((B, Lp - L, D), x.dtype)
    h = jnp.concatenate([meta, x, pad], axis=1)
    pos = jnp.arange(Lp, dtype=jnp.int32)
    for l in range(DEPTH):
        n = rmsnorm(h, g_attn_pre[l])
        a = token_mixer(n, pos, rel_bias, w_in[l], w_uq[l], w_ukv[l], w_o[l], g_cq[l], g_ckv[l],
                        lambda_q1[l], lambda_k1[l], lambda_q2[l], lambda_k2[l], g_diff_sub[l], l)
        h = h + rmsnorm(a, g_attn_post[l])
        n = rmsnorm(h, g_ffn_pre[l])
        f = conv_gated_mlp(n, w_up[l], conv_w[l], conv_b[l], w_down[l])
        h = h + rmsnorm(f, g_ffn_post[l])
    return h[:, N_META:N_META + S]
```

```python
import functools
import math

import numpy as np
import jax
import jax.numpy as jnp
from jax import lax
from jax.experimental import pallas as pl
from jax.experimental.pallas import tpu as pltpu

F32 = jnp.float32
BF16 = jnp.bfloat16

N_META = 16
MLA_HEADS = 8
MLA_NOPE = 128
MLA_ROPE = 64
MLA_V = 128
MLA_QK = MLA_NOPE + MLA_ROPE
Q_LORA = 512
KV_LORA = 256
ROPE_THETA = 10000.0
DIFF_HEADS = 4
DIFF_DK = 128
DIFF_V = 2 * DIFF_DK
DIFF_QK = DIFF_HEADS * 2 * DIFF_DK
NUM_BUCKETS = 32
REL_MAX_DIST = 128
CONV_W = 3
EPS = 1e-6
LOG2E = math.log2(math.e)

TILE = 256
META_KEYS = 128
FFN_TM = 512
FFN_TF = 512
FFN_HALO = 16
VMEM_LIMIT = 56 * 1024 * 1024


def _rms(x, g):
    ms = jnp.mean(x * x, axis=-1, keepdims=True)
    return x * lax.rsqrt(ms + EPS) * g


def _dot(a, b):
    return jnp.dot(a, b, preferred_element_type=F32)


def _dot_nt(a, b):
    return lax.dot_general(a, b, (((1,), (1,)), ((), ())), preferred_element_type=F32)


def _proj_body(n_x_tiles, q_scale_mla, q_scale_diff,
               x_ref, meta_ref, cos_ref, sin_ref, cs_ref, g_pre_ref, w_in_ref, g_cq_ref, g_ckv_ref,
               w_uq_ref, w_ukv_ref,
               qm_ref, km_ref, vmt_ref, qd_ref, kd_ref, vdt_ref,
               n_scr, proj_scr, qm_scr, kv_scr):
    m = pl.program_id(0)

    @pl.when(m < n_x_tiles)
    def _():
        n_scr[...] = _rms(x_ref[...], g_pre_ref[...]).astype(BF16)

    @pl.when(m == n_x_tiles)
    def _():
        n_scr[...] = _rms(meta_ref[...], g_pre_ref[...]).astype(BF16)

    proj_scr[...] = _dot(n_scr[...], w_in_ref[...])
    c_q = _rms(proj_scr[:, 0:Q_LORA], g_cq_ref[...]).astype(BF16)
    c_kv = _rms(proj_scr[:, Q_LORA:Q_LORA + KV_LORA], g_ckv_ref[...]).astype(BF16)
    qm_scr[...] = _dot(c_q, w_uq_ref[...])
    kv_scr[...] = _dot(c_kv, w_ukv_ref[...])

    n_nope = MLA_HEADS * MLA_NOPE
    n_rope = MLA_HEADS * MLA_ROPE
    q_rope = (qm_scr[:, n_nope:n_nope + n_rope] * cos_ref[...]
              + qm_scr[:, n_nope + n_rope:n_nope + 2 * n_rope] * sin_ref[...])
    off_kr = Q_LORA + KV_LORA + 2 * DIFF_QK + DIFF_HEADS * DIFF_V
    kt = proj_scr[:, off_kr:off_kr + 2 * MLA_ROPE] * cs_ref[...]
    k_rope = (kt[:, 0:MLA_ROPE] + kt[:, MLA_ROPE:2 * MLA_ROPE]).astype(BF16)

    for h in range(MLA_HEADS):
        qm_ref[h, :, 0:MLA_NOPE] = (qm_scr[:, h * MLA_NOPE:(h + 1) * MLA_NOPE] * q_scale_mla).astype(BF16)
        qm_ref[h, :, MLA_NOPE:MLA_QK] = (q_rope[:, h * MLA_ROPE:(h + 1) * MLA_ROPE] * q_scale_mla).astype(BF16)
        km_ref[h, 0, :, 0:MLA_NOPE] = kv_scr[:, h * MLA_NOPE:(h + 1) * MLA_NOPE].astype(BF16)
        km_ref[h, 0, :, MLA_NOPE:MLA_QK] = k_rope
        v = kv_scr[:, n_nope + h * MLA_V:n_nope + (h + 1) * MLA_V]
        vmt_ref[h, 0] = v.T.astype(BF16)

    off_qd = Q_LORA + KV_LORA
    off_kd = off_qd + DIFF_QK
    off_vd = off_kd + DIFF_QK
    for i in range(2 * DIFF_HEADS):
        qd_ref[i] = (proj_scr[:, off_qd + i * DIFF_DK:off_qd + (i + 1) * DIFF_DK] * q_scale_diff).astype(BF16)
        kd_ref[i, 0] = proj_scr[:, off_kd + i * DIFF_DK:off_kd + (i + 1) * DIFF_DK].astype(BF16)
    for h in range(DIFF_HEADS):
        v = proj_scr[:, off_vd + h * DIFF_V:off_vd + (h + 1) * DIFF_V]
        vdt_ref[h, 0] = v.T.astype(BF16)


def _mla_body(n_x_tiles, tiles_per_batch,
              q_ref, k_ref, vt_ref, kmeta_ref, vtmeta_ref, o_ref, m_scr, l_scr, acc_scr):
    s = pl.program_id(1)
    is_x = s < n_x_tiles
    t = lax.rem(s, tiles_per_batch)
    q = q_ref[...]

    row = lax.broadcasted_iota(jnp.int32, (META_KEYS, TILE), 0)
    col = lax.broadcasted_iota(jnp.int32, (META_KEYS, TILE), 1)
    limit = jnp.where(is_x, N_META - 1, jnp.minimum(col, N_META - 1))
    st = jnp.where(row <= limit, _dot_nt(kmeta_ref[0:META_KEYS, :], q), -jnp.inf)
    m0 = jnp.max(st, axis=0, keepdims=True)
    p = jnp.exp2(st - m0)
    m_scr[...] = m0
    l_scr[...] = jnp.sum(p, axis=0, keepdims=True)
    acc_scr[...] = _dot(vtmeta_ref[:, 0:META_KEYS], p.astype(BF16))

    def step(k_blk, vt_blk, mask):
        st = _dot_nt(k_blk, q)
        if mask is not None:
            st = jnp.where(mask, st, -jnp.inf)
        m_prev = m_scr[...]
        m_new = jnp.maximum(m_prev, jnp.max(st, axis=0, keepdims=True))
        alpha = jnp.exp2(m_prev - m_new)
        p = jnp.exp2(st - m_new)
        l_scr[...] = alpha * l_scr[...] + jnp.sum(p, axis=0, keepdims=True)
        acc_scr[...] = alpha * acc_scr[...] + _dot(vt_blk, p.astype(BF16))
        m_scr[...] = m_new

    def full_chunk(j, carry):
        step(k_ref[j], vt_ref[j], None)
        return carry

    lax.fori_loop(0, jnp.where(is_x, t, 0), full_chunk, 0)

    @pl.when(is_x)
    def _():
        rowd = lax.broadcasted_iota(jnp.int32, (TILE, TILE), 0)
        cold = lax.broadcasted_iota(jnp.int32, (TILE, TILE), 1)
        step(k_ref[t], vt_ref[t], rowd <= cold)

    o_t = acc_scr[...] / l_scr[...]
    o_ref[...] = o_t.T.astype(BF16)


def _bias_body(idx_ref, rel_ref, out_ref):
    idx = idx_ref[0]
    n_maps = out_ref.shape[1]
    for mp in range(n_maps):
        far = jnp.full(idx.shape, rel_ref[NUM_BUCKETS - 1, mp], F32)
        acc = far
        for b in range(NUM_BUCKETS - 1):
            acc = jnp.where(idx == b, rel_ref[b, mp], acc)
        out_ref[0, mp] = (acc - far) * LOG2E


def _diff_body(n_x_tiles, tiles_per_batch, lam_init,
               q_ref, k_ref, vt_ref, kmeta_ref, vtmeta_ref, bias_ref,
               lq1_ref, lk1_ref, lq2_ref, lk2_ref, gsub_ref, o_ref,
               m_scr, l_scr, acc_scr):
    s = pl.program_id(1)
    is_x = s < n_x_tiles
    t = lax.rem(s, tiles_per_batch)

    row = lax.broadcasted_iota(jnp.int32, (META_KEYS, TILE), 0)
    col = lax.broadcasted_iota(jnp.int32, (META_KEYS, TILE), 1)
    limit = jnp.where(is_x, N_META - 1, jnp.minimum(col, N_META - 1))
    meta_valid = row <= limit
    w_first = jnp.where(jnp.logical_and(is_x, t == 0), 1.0, 0.0).astype(F32)
    w_meta = jnp.where(is_x, 0.0, 1.0).astype(F32)
    for c in range(2):
        q = q_ref[c]
        b = w_first * bias_ref[2, c, 0:META_KEYS, :] + w_meta * bias_ref[0, c, 0:META_KEYS, :]
        st = jnp.where(meta_valid, _dot_nt(kmeta_ref[c, 0:META_KEYS, :], q) + b, -jnp.inf)
        m0 = jnp.max(st, axis=0, keepdims=True)
        p = jnp.exp2(st - m0)
        m_scr[c] = m0
        l_scr[c] = jnp.sum(p, axis=0, keepdims=True)
        acc_scr[c] = _dot(vtmeta_ref[:, 0:META_KEYS], p.astype(BF16))

    def step(j, bias_idx, mask):
        vt_blk = vt_ref[j]
        for c in range(2):
            st = _dot_nt(k_ref[c, j], q_ref[c])
            if bias_idx is not None:
                st = st + bias_ref[bias_idx, c]
            if mask is not None:
                st = jnp.where(mask, st, -jnp.inf)
            m_prev = m_scr[c]
            m_new = jnp.maximum(m_prev, jnp.max(st, axis=0, keepdims=True))
            alpha = jnp.exp2(m_prev - m_new)
            p = jnp.exp2(st - m_new)
            l_scr[c] = alpha * l_scr[c] + jnp.sum(p, axis=0, keepdims=True)
            acc_scr[c] = alpha * acc_scr[c] + _dot(vt_blk, p.astype(BF16))
            m_scr[c] = m_new

    def far_chunk(j, carry):
        step(j, None, None)
        return carry

    lax.fori_loop(0, jnp.where(is_x, jnp.maximum(t - 1, 0), 0), far_chunk, 0)

    @pl.when(jnp.logical_and(is_x, t >= 1))
    def _():
        step(t - 1, 1, None)

    @pl.when(is_x)
    def _():
        rowd = lax.broadcasted_iota(jnp.int32, (TILE, TILE), 0)
        cold = lax.broadcasted_iota(jnp.int32, (TILE, TILE), 1)
        step(t, 0, rowd <= cold)

    lam = (jnp.exp(jnp.sum(lq1_ref[...] * lk1_ref[...], axis=-1, keepdims=True))
           - jnp.exp(jnp.sum(lq2_ref[...] * lk2_ref[...], axis=-1, keepdims=True)) + lam_init)
    o_t = acc_scr[0] / l_scr[0] - lam * (acc_scr[1] / l_scr[1])
    o = o_t.T
    o_ref[...] = (_rms(o, gsub_ref[...]) * (1.0 - lam_init)).astype(BF16)


def _oproj_body(n_x_tiles, om_ref, od_ref, wo_ref, x_ref, meta_ref, g_post_ref, g_pre_ref, h1_ref, n2_ref):
    m = pl.program_id(0)
    n_m = om_ref.shape[1]
    a = _dot(om_ref[...], wo_ref[0:n_m, :]) + _dot(od_ref[...], wo_ref[n_m:, :])
    h1_ref[...] = _rms(a, g_post_ref[...])

    @pl.when(m < n_x_tiles)
    def _():
        h1_ref[...] += x_ref[...]

    @pl.when(m == n_x_tiles)
    def _():
        h1_ref[...] += meta_ref[...]

    n2_ref[...] = _rms(h1_ref[...], g_pre_ref[...]).astype(BF16)


def _ffn_body(n_f,
              n2_ref, halo_ref, h1_ref, wg_ref, wv_ref, wd_ref, cwg_ref, cwv_ref, cbg_ref, cbv_ref, g_ref,
              out_ref, n2e_scr, up_scr):
    f = pl.program_id(1)
    tm = n2_ref.shape[0]

    @pl.when(f == 0)
    def _():
        n2e_scr[0:FFN_HALO, :] = halo_ref[...]
        n2e_scr[FFN_HALO:, :] = n2_ref[...]
        out_ref[...] = jnp.zeros(out_ref.shape, F32)

    def conv_branch(w_ref, cw_ref, cb_ref):
        up_scr[...] = _dot(n2e_scr[...], w_ref[...])
        acc = cb_ref[...] + cw_ref[0:1, :] * up_scr[FFN_HALO - 2:FFN_HALO - 2 + tm, :]
        acc = acc + cw_ref[1:2, :] * up_scr[FFN_HALO - 1:FFN_HALO - 1 + tm, :]
        return acc + cw_ref[2:3, :] * up_scr[FFN_HALO:FFN_HALO + tm, :]

    gate = conv_branch(wg_ref, cwg_ref, cbg_ref)
    val = conv_branch(wv_ref, cwv_ref, cbv_ref)
    act = (gate / (1.0 + jnp.exp(-gate)) * val).astype(BF16)
    out_ref[...] += _dot(act, wd_ref[...])

    @pl.when(f == n_f - 1)
    def _():
        out_ref[...] = h1_ref[...] + _rms(out_ref[...], g_ref[...])


def _t5_bucket_np(n):
    n = np.maximum(n, 0)
    max_exact = NUM_BUCKETS // 2
    nf = np.maximum(n, 1).astype(np.float32)
    large = max_exact + (np.log(nf / np.float32(max_exact)) / np.float32(math.log(REL_MAX_DIST / max_exact))
                         * np.float32(NUM_BUCKETS - max_exact)).astype(np.int32)
    large = np.minimum(large, NUM_BUCKETS - 1)
    return np.where(n < max_exact, n, large).astype(np.int32)


def _bias_patterns():
    kk = np.arange(TILE)[:, None]
    qq = np.arange(TILE)[None, :]
    return np.stack([_t5_bucket_np(qq - kk), _t5_bucket_np(TILE + qq - kk), _t5_bucket_np(qq + N_META - kk)])


def kernel(x, meta_tokens, rel_bias, w_in, w_uq, w_ukv, w_o, g_attn_pre, g_attn_post, g_cq, g_ckv,
           lambda_q1, lambda_k1, lambda_q2, lambda_k2, g_diff_sub, g_ffn_pre, g_ffn_post,
           w_up, conv_w, conv_b, w_down):
    B, S, D = x.shape
    assert w_in.shape[0] == 1, "single-layer problem"
    assert S % TILE == 0 and S % FFN_TM == 0
    d_ff = w_down.shape[1]
    assert d_ff % FFN_TF == 0
    tpb = S // TILE
    nxt = B * tpb
    nt = nxt + 1
    R = nt * TILE
    n_bias_maps = rel_bias.shape[1]
    lam_init = 0.8 - 0.6 * math.exp(-0.3 * 0)
    d_mix = MLA_HEADS * MLA_V + DIFF_HEADS * DIFF_V

    def rot_cols(w):
        half = w.shape[-1] // 2
        return jnp.concatenate([-w[..., half:], w[..., :half]], axis=-1)

    wi = w_in[0]
    c0 = Q_LORA + KV_LORA
    k_r = wi[:, c0:c0 + MLA_ROPE]
    w_in_p = jnp.concatenate([wi[:, :c0], wi[:, c0 + MLA_ROPE:], k_r, rot_cols(k_r)], axis=1).astype(BF16)
    wq = w_uq[0].reshape(Q_LORA, MLA_HEADS, MLA_QK)
    q_rope_w = wq[:, :, MLA_NOPE:]
    w_uq_p = jnp.concatenate([wq[:, :, :MLA_NOPE].reshape(Q_LORA, -1), q_rope_w.reshape(Q_LORA, -1),
                              rot_cols(q_rope_w).reshape(Q_LORA, -1)], axis=1).astype(BF16)
    wkv = w_ukv[0].reshape(KV_LORA, MLA_HEADS, MLA_NOPE + MLA_V)
    w_ukv_p = jnp.concatenate([wkv[:, :, :MLA_NOPE].reshape(KV_LORA, -1),
                               wkv[:, :, MLA_NOPE:].reshape(KV_LORA, -1)], axis=1).astype(BF16)
    w_o_b = w_o[0].astype(BF16)
    w_up_b = w_up[0].astype(BF16)
    w_down_b = w_down[0].astype(BF16)

    half = MLA_ROPE // 2
    inv = ROPE_THETA ** (-jnp.arange(half, dtype=F32) / half)
    pos = jnp.concatenate([jnp.arange(S, dtype=jnp.int32) + N_META, jnp.arange(TILE, dtype=jnp.int32)])
    ang = pos.astype(F32)[:, None] * inv[None, :]
    cos64 = jnp.tile(jnp.cos(ang), (1, 2))
    sin64 = jnp.tile(jnp.sin(ang), (1, 2))
    cos_q = jnp.tile(cos64, (1, MLA_HEADS))
    sin_q = jnp.tile(sin64, (1, MLA_HEADS))
    cs_k = jnp.concatenate([cos64, sin64], axis=1)

    x2 = x.reshape(B * S, D)
    meta_pad = jnp.concatenate([meta_tokens.astype(x.dtype), jnp.zeros((TILE - N_META, D), x.dtype)], axis=0)
    row1 = lambda v: v.reshape(1, -1)

    x_idx = lambda m: (jnp.minimum(m, nxt - 1), 0)
    tab_idx = lambda m: (jnp.where(m < nxt, lax.rem(m, tpb), tpb), 0)
    const2 = lambda m: (0, 0)
    d_in_p = w_in_p.shape[1]
    resident = dict(pipeline_mode=pl.Buffered(1))

    qm, km, vmt, qd, kd, vdt = pl.pallas_call(
        functools.partial(_proj_body, nxt, LOG2E / math.sqrt(MLA_QK), LOG2E / math.sqrt(DIFF_DK)),
        grid=(nt,),
        in_specs=[
            pl.BlockSpec((TILE, D), x_idx),
            pl.BlockSpec((TILE, D), const2),
            pl.BlockSpec((TILE, MLA_HEADS * MLA_ROPE), tab_idx),
            pl.BlockSpec((TILE, MLA_HEADS * MLA_ROPE), tab_idx),
            pl.BlockSpec((TILE, 2 * MLA_ROPE), tab_idx),
            pl.BlockSpec((1, D), const2),
            pl.BlockSpec((D, d_in_p), const2, **resident),
            pl.BlockSpec((1, Q_LORA), const2),
            pl.BlockSpec((1, KV_LORA), const2),
            pl.BlockSpec(w_uq_p.shape, const2, **resident),
            pl.BlockSpec(w_ukv_p.shape, const2, **resident),
        ],
        out_specs=[
            pl.BlockSpec((MLA_HEADS, TILE, MLA_QK), lambda m: (0, m, 0)),
            pl.BlockSpec((MLA_HEADS, 1, TILE, MLA_QK), lambda m: (0, m, 0, 0)),
            pl.BlockSpec((MLA_HEADS, 1, MLA_V, TILE), lambda m: (0, m, 0, 0)),
            pl.BlockSpec((2 * DIFF_HEADS, TILE, DIFF_DK), lambda m: (0, m, 0)),
            pl.BlockSpec((2 * DIFF_HEADS, 1, TILE, DIFF_DK), lambda m: (0, m, 0, 0)),
            pl.BlockSpec((DIFF_HEADS, 1, DIFF_V, TILE), lambda m: (0, m, 0, 0)),
        ],
        out_shape=[
            jax.ShapeDtypeStruct((MLA_HEADS, R, MLA_QK), BF16),
            jax.ShapeDtypeStruct((MLA_HEADS, nt, TILE, MLA_QK), BF16),
            jax.ShapeDtypeStruct((MLA_HEADS, nt, MLA_V, TILE), BF16),
            jax.ShapeDtypeStruct((2 * DIFF_HEADS, R, DIFF_DK), BF16),
            jax.ShapeDtypeStruct((2 * DIFF_HEADS, nt, TILE, DIFF_DK), BF16),
            jax.ShapeDtypeStruct((DIFF_HEADS, nt, DIFF_V, TILE), BF16),
        ],
        scratch_shapes=[
            pltpu.VMEM((TILE, D), BF16),
            pltpu.VMEM((TILE, d_in_p), F32),
            pltpu.VMEM((TILE, w_uq_p.shape[1]), F32),
            pltpu.VMEM((TILE, w_ukv_p.shape[1]), F32),
        ],
        compiler_params=pltpu.CompilerParams(dimension_semantics=("arbitrary",), vmem_limit_bytes=VMEM_LIMIT),
        name="in_proj",
    )(x2, meta_pad, cos_q, sin_q, cs_k, row1(g_attn_pre[0]), w_in_p, row1(g_cq[0]), row1(g_ckv[0]),
      w_uq_p, w_ukv_p)

    batch_of = lambda s: jnp.minimum(s // tpb, B - 1)

    o_m = pl.pallas_call(
        functools.partial(_mla_body, nxt, tpb),
        grid=(MLA_HEADS, nt),
        in_specs=[
            pl.BlockSpec((None, TILE, MLA_QK), lambda h, s: (h, s, 0)),
            pl.BlockSpec((None, tpb, TILE, MLA_QK), lambda h, s: (h, batch_of(s), 0, 0)),
            pl.BlockSpec((None, tpb, MLA_V, TILE), lambda h, s: (h, batch_of(s), 0, 0)),
            pl.BlockSpec((None, None, TILE, MLA_QK), lambda h, s: (h, nxt, 0, 0)),
            pl.BlockSpec((None, None, MLA_V, TILE), lambda h, s: (h, nxt, 0, 0)),
        ],
        out_specs=pl.BlockSpec((TILE, MLA_V), lambda h, s: (s, h)),
        out_shape=jax.ShapeDtypeStruct((R, MLA_HEADS * MLA_V), BF16),
        scratch_shapes=[
            pltpu.VMEM((1, TILE), F32),
            pltpu.VMEM((1, TILE), F32),
            pltpu.VMEM((MLA_V, TILE), F32),
        ],
        compiler_params=pltpu.CompilerParams(dimension_semantics=("arbitrary", "arbitrary")),
        name="mla_attn",
    )(qm, km, vmt, km, vmt)

    patterns = jnp.asarray(_bias_patterns())
    bias_tiles = pl.pallas_call(
        _bias_body,
        grid=(patterns.shape[0],),
        in_specs=[
            pl.BlockSpec((1, TILE, TILE), lambda p: (p, 0, 0)),
            pl.BlockSpec(memory_space=pltpu.SMEM),
        ],
        out_specs=pl.BlockSpec((1, n_bias_maps, TILE, TILE), lambda p: (p, 0, 0, 0)),
        out_shape=jax.ShapeDtypeStruct((patterns.shape[0], n_bias_maps, TILE, TILE), F32),
        compiler_params=pltpu.CompilerParams(dimension_semantics=("arbitrary",)),
        name="rel_bias_tiles",
    )(patterns, rel_bias.astype(F32))

    o_d = pl.pallas_call(
        functools.partial(_diff_body, nxt, tpb, lam_init),
        grid=(DIFF_HEADS, nt),
        in_specs=[
            pl.BlockSpec((2, TILE, DIFF_DK), lambda h, s: (h, s, 0)),
            pl.BlockSpec((2, tpb, TILE, DIFF_DK), lambda h, s: (h, batch_of(s), 0, 0)),
            pl.BlockSpec((None, tpb, DIFF_V, TILE), lambda h, s: (h, batch_of(s), 0, 0)),
            pl.BlockSpec((2, None, TILE, DIFF_DK), lambda h, s: (h, nxt, 0, 0)),
            pl.BlockSpec((None, None, DIFF_V, TILE), lambda h, s: (h, nxt, 0, 0)),
            pl.BlockSpec((patterns.shape[0], 2, TILE, TILE), lambda h, s: (0, h, 0, 0)),
            pl.BlockSpec((1, DIFF_DK), lambda h, s: (0, 0)),
            pl.BlockSpec((1, DIFF_DK), lambda h, s: (0, 0)),
            pl.BlockSpec((1, DIFF_DK), lambda h, s: (0, 0)),
            pl.BlockSpec((1, DIFF_DK), lambda h, s: (0, 0)),
            pl.BlockSpec((1, DIFF_V), lambda h, s: (0, 0)),
        ],
        out_specs=pl.BlockSpec((TILE, DIFF_V), lambda h, s: (s, h)),
        out_shape=jax.ShapeDtypeStruct((R, DIFF_HEADS * DIFF_V), BF16),
        scratch_shapes=[
            pltpu.VMEM((2, 1, TILE), F32),
            pltpu.VMEM((2, 1, TILE), F32),
            pltpu.VMEM((2, DIFF_V, TILE), F32),
        ],
        compiler_params=pltpu.CompilerParams(dimension_semantics=("arbitrary", "arbitrary")),
        name="diff_attn",
    )(qd, kd, vdt, kd, vdt, bias_tiles, lambda_q1.astype(F32), lambda_k1.astype(F32),
      lambda_q2.astype(F32), lambda_k2.astype(F32), row1(g_diff_sub[0]))

    h1, n2 = pl.pallas_call(
        functools.partial(_oproj_body, nxt),
        grid=(nt,),
        in_specs=[
            pl.BlockSpec((TILE, MLA_HEADS * MLA_V), lambda m: (m, 0)),
            pl.BlockSpec((TILE, DIFF_HEADS * DIFF_V), lambda m: (m, 0)),
            pl.BlockSpec((d_mix, D), const2, **resident),
            pl.BlockSpec((TILE, D), x_idx),
            pl.BlockSpec((TILE, D), const2),
            pl.BlockSpec((1, D), const2),
            pl.BlockSpec((1, D), const2),
        ],
        out_specs=[
            pl.BlockSpec((TILE, D), lambda m: (m, 0)),
            pl.BlockSpec((TILE, D), lambda m: (m, 0)),
        ],
        out_shape=[
            jax.ShapeDtypeStruct((R, D), F32),
            jax.ShapeDtypeStruct((R, D), BF16),
        ],
        compiler_params=pltpu.CompilerParams(dimension_semantics=("arbitrary",), vmem_limit_bytes=VMEM_LIMIT),
        name="out_proj",
    )(o_m, o_d, w_o_b, x2, meta_pad, row1(g_attn_post[0]), row1(g_ffn_pre[0]))

    n_m = (B * S) // FFN_TM
    n_f = d_ff // FFN_TF
    tiles_per_batch_ffn = S // FFN_TM
    halo_per_tile = FFN_TM // FFN_HALO
    meta_halo_blk = (B * S) // FFN_HALO

    def halo_idx(m, f):
        return (jnp.where(lax.rem(m, tiles_per_batch_ffn) == 0, meta_halo_blk, m * halo_per_tile - 1), 0)

    out = pl.pallas_call(
        functools.partial(_ffn_body, n_f),
        grid=(n_m, n_f),
        in_specs=[
            pl.BlockSpec((FFN_TM, D), lambda m, f: (m, 0)),
            pl.BlockSpec((FFN_HALO, D), halo_idx),
            pl.BlockSpec((FFN_TM, D), lambda m, f: (m, 0)),
            pl.BlockSpec((D, FFN_TF), lambda m, f: (0, f)),
            pl.BlockSpec((D, FFN_TF), lambda m, f: (0, n_f + f)),
            pl.BlockSpec((FFN_TF, D), lambda m, f: (f, 0)),
            pl.BlockSpec((CONV_W, FFN_TF), lambda m, f: (0, f)),
            pl.BlockSpec((CONV_W, FFN_TF), lambda m, f: (0, n_f + f)),
            pl.BlockSpec((1, FFN_TF), lambda m, f: (0, f)),
            pl.BlockSpec((1, FFN_TF), lambda m, f: (0, n_f + f)),
            pl.BlockSpec((1, D), lambda m, f: (0, 0)),
        ],
        out_specs=pl.BlockSpec((FFN_TM, D), lambda m, f: (m, 0)),
        out_shape=jax.ShapeDtypeStruct((B * S, D), x.dtype),
        scratch_shapes=[
            pltpu.VMEM((FFN_HALO + FFN_TM, D), BF16),
            pltpu.VMEM((FFN_HALO + FFN_TM, FFN_TF), F32),
        ],
        compiler_params=pltpu.CompilerParams(dimension_semantics=("arbitrary", "arbitrary"),
                                             vmem_limit_bytes=VMEM_LIMIT),
        name="conv_ffn",
    )(n2, n2, h1, w_up_b, w_up_b, w_down_b, conv_w[0], conv_w[0], row1(conv_b[0]), row1(conv_b[0]),
      row1(g_ffn_post[0]))

    return out.reshape(B, S, D)
```

```python
import functools
import math

import numpy as np
import jax
import jax.numpy as jnp
from jax import lax
from jax.experimental import pallas as pl
from jax.experimental.pallas import tpu as pltpu

F32 = jnp.float32
BF16 = jnp.bfloat16

N_META = 16
MLA_HEADS = 8
MLA_NOPE = 128
MLA_ROPE = 64
MLA_V = 128
MLA_QK = MLA_NOPE + MLA_ROPE
Q_LORA = 512
KV_LORA = 256
ROPE_THETA = 10000.0
DIFF_HEADS = 4
DIFF_DK = 128
DIFF_V = 2 * DIFF_DK
DIFF_QK = DIFF_HEADS * 2 * DIFF_DK
NUM_BUCKETS = 32
REL_MAX_DIST = 128
CONV_W = 3
EPS = 1e-6
LOG2E = math.log2(math.e)

TILE = 256
META_KEYS = 128
MLA_GROUP = 4
DIFF_GROUP = 2
FFN_TM = 512
FFN_TF = 512
FFN_HALO = 16
VMEM_LIMIT = 56 * 1024 * 1024


def _rms(x, g):
    ms = jnp.mean(x * x, axis=-1, keepdims=True)
    return x * lax.rsqrt(ms + EPS) * g


def _dot(a, b):
    return jnp.dot(a, b, preferred_element_type=F32)


def _dot_nt(a, b):
    return lax.dot_general(a, b, (((1,), (1,)), ((), ())), preferred_element_type=F32)


def _proj_body(n_x_tiles, q_scale_mla, q_scale_diff,
               x_ref, meta_ref, cos_ref, sin_ref, cs_ref, g_pre_ref, w_in_ref, g_cq_ref, g_ckv_ref,
               w_uq_ref, w_ukv_ref,
               qm_ref, km_ref, vmt_ref, qd_ref, kd_ref, vdt_ref,
               n_scr, proj_scr, qm_scr, kv_scr):
    m = pl.program_id(0)

    @pl.when(m < n_x_tiles)
    def _():
        n_scr[...] = _rms(x_ref[...], g_pre_ref[...]).astype(BF16)

    @pl.when(m == n_x_tiles)
    def _():
        n_scr[...] = _rms(meta_ref[...], g_pre_ref[...]).astype(BF16)

    proj_scr[...] = _dot(n_scr[...], w_in_ref[...])
    c_q = _rms(proj_scr[:, 0:Q_LORA], g_cq_ref[...]).astype(BF16)
    c_kv = _rms(proj_scr[:, Q_LORA:Q_LORA + KV_LORA], g_ckv_ref[...]).astype(BF16)
    qm_scr[...] = _dot(c_q, w_uq_ref[...])
    kv_scr[...] = _dot(c_kv, w_ukv_ref[...])

    n_nope = MLA_HEADS * MLA_NOPE
    n_rope = MLA_HEADS * MLA_ROPE
    q_rope = (qm_scr[:, n_nope:n_nope + n_rope] * cos_ref[...]
              + qm_scr[:, n_nope + n_rope:n_nope + 2 * n_rope] * sin_ref[...])
    off_kr = Q_LORA + KV_LORA + 2 * DIFF_QK + DIFF_HEADS * DIFF_V
    kt = proj_scr[:, off_kr:off_kr + 2 * MLA_ROPE] * cs_ref[...]
    k_rope = (kt[:, 0:MLA_ROPE] + kt[:, MLA_ROPE:2 * MLA_ROPE]).astype(BF16)

    for h in range(MLA_HEADS):
        qm_ref[h, :, 0:MLA_NOPE] = (qm_scr[:, h * MLA_NOPE:(h + 1) * MLA_NOPE] * q_scale_mla).astype(BF16)
        qm_ref[h, :, MLA_NOPE:MLA_QK] = (q_rope[:, h * MLA_ROPE:(h + 1) * MLA_ROPE] * q_scale_mla).astype(BF16)
        km_ref[h, 0, :, 0:MLA_NOPE] = kv_scr[:, h * MLA_NOPE:(h + 1) * MLA_NOPE].astype(BF16)
        km_ref[h, 0, :, MLA_NOPE:MLA_QK] = k_rope
        v = kv_scr[:, n_nope + h * MLA_V:n_nope + (h + 1) * MLA_V]
        vmt_ref[h, 0] = v.T.astype(BF16)

    off_qd = Q_LORA + KV_LORA
    off_kd = off_qd + DIFF_QK
    off_vd = off_kd + DIFF_QK
    for i in range(2 * DIFF_HEADS):
        qd_ref[i] = (proj_scr[:, off_qd + i * DIFF_DK:off_qd + (i + 1) * DIFF_DK] * q_scale_diff).astype(BF16)
        kd_ref[i, 0] = proj_scr[:, off_kd + i * DIFF_DK:off_kd + (i + 1) * DIFF_DK].astype(BF16)
    for h in range(DIFF_HEADS):
        v = proj_scr[:, off_vd + h * DIFF_V:off_vd + (h + 1) * DIFF_V]
        vdt_ref[h, 0] = v.T.astype(BF16)


def _chunk_pipeline(is_x, t, qk_meta, qk_x, update_meta, update_x, b0, b1):
    qk_meta(b0)

    @pl.when(jnp.logical_not(is_x))
    def _():
        update_meta(b0)

    @pl.when(is_x)
    def _():
        qk_x(b1, t, "diag")
        update_meta(b0)

        @pl.when(t == 0)
        def _():
            update_x(b1, t)

        @pl.when(t >= 1)
        def _():
            qk_x(b0, t - 1, "prev")
            update_x(b1, t)
            n_far = t - 1
            n_pairs = n_far // 2

            def pair(k, carry):
                qk_x(b1, 2 * k, "far")
                update_x(b0, jnp.where(k == 0, t - 1, 2 * k - 1))
                qk_x(b0, 2 * k + 1, "far")
                update_x(b1, 2 * k)
                return carry

            lax.fori_loop(0, n_pairs, pair, 0)
            j_pending = jnp.where(n_pairs == 0, t - 1, 2 * n_pairs - 1)

            @pl.when(n_far % 2 == 1)
            def _():
                qk_x(b1, n_far - 1, "far")
                update_x(b0, j_pending)
                update_x(b1, n_far - 1)

            @pl.when(n_far % 2 == 0)
            def _():
                update_x(b0, j_pending)


def _mla_body(n_x_tiles, tiles_per_batch,
              q_ref, k_ref, vt_ref, kmeta_ref, vtmeta_ref, o_ref, m_scr, l_scr, acc_scr, st0_scr, st1_scr):
    n_heads = q_ref.shape[0]
    s = pl.program_id(1)
    is_x = s < n_x_tiles
    t = lax.rem(s, tiles_per_batch)

    for g in range(n_heads):
        m_scr[g] = jnp.full((1, TILE), -jnp.inf, F32)
        l_scr[g] = jnp.zeros((1, TILE), F32)
        acc_scr[g] = jnp.zeros((MLA_V, TILE), F32)

    def qk_meta(buf):
        row = lax.broadcasted_iota(jnp.int32, (META_KEYS, TILE), 0)
        col = lax.broadcasted_iota(jnp.int32, (META_KEYS, TILE), 1)
        limit = jnp.where(is_x, N_META - 1, jnp.minimum(col, N_META - 1))
        for g in range(n_heads):
            st = _dot_nt(kmeta_ref[g, 0:META_KEYS, :], q_ref[g])
            buf[g, 0:META_KEYS, :] = jnp.where(row <= limit, st, -jnp.inf)

    def qk_x(buf, j, kind):
        if kind == "diag":
            rowd = lax.broadcasted_iota(jnp.int32, (TILE, TILE), 0)
            cold = lax.broadcasted_iota(jnp.int32, (TILE, TILE), 1)
        for g in range(n_heads):
            st = _dot_nt(k_ref[g, j], q_ref[g])
            if kind == "diag":
                st = jnp.where(rowd <= cold, st, -jnp.inf)
            buf[g] = st

    def update(buf, rows, vt_of):
        for g in range(n_heads):
            m_prev = m_scr[g]
            m_new = jnp.maximum(m_prev, jnp.max(buf[g, 0:rows, :], axis=0, keepdims=True))
            alpha = jnp.exp2(m_prev - m_new)
            p = jnp.exp2(buf[g, 0:rows, :] - m_new)
            l_scr[g] = alpha * l_scr[g] + jnp.sum(p, axis=0, keepdims=True)
            acc_scr[g] = alpha * acc_scr[g] + _dot(vt_of(g), p.astype(BF16))
            m_scr[g] = m_new

    def update_meta(buf):
        update(buf, META_KEYS, lambda g: vtmeta_ref[g, :, 0:META_KEYS])

    def update_x(buf, j):
        update(buf, TILE, lambda g: vt_ref[g, j])

    _chunk_pipeline(is_x, t, qk_meta, qk_x, update_meta, update_x, st0_scr, st1_scr)

    for g in range(n_heads):
        o_t = acc_scr[g] / l_scr[g]
        o_ref[:, g * MLA_V:(g + 1) * MLA_V] = o_t.T.astype(BF16)


def _bias_body(idx_ref, rel_ref, out_ref):
    idx = idx_ref[0]
    n_maps = out_ref.shape[1]
    for mp in range(n_maps):
        far = jnp.full(idx.shape, rel_ref[NUM_BUCKETS - 1, mp], F32)
        acc = far
        for b in range(NUM_BUCKETS - 1):
            acc = jnp.where(idx == b, rel_ref[b, mp], acc)
        out_ref[0, mp] = (acc - far) * LOG2E


def _diff_body(n_x_tiles, tiles_per_batch, lam_init,
               q_ref, k_ref, vt_ref, kmeta_ref, vtmeta_ref, bias_ref,
               lq1_ref, lk1_ref, lq2_ref, lk2_ref, gsub_ref, o_ref,
               m_scr, l_scr, acc_scr, st0_scr, st1_scr):
    n_maps = q_ref.shape[0]
    s = pl.program_id(1)
    is_x = s < n_x_tiles
    t = lax.rem(s, tiles_per_batch)

    for i in range(n_maps):
        m_scr[i] = jnp.full((1, TILE), -jnp.inf, F32)
        l_scr[i] = jnp.zeros((1, TILE), F32)
        acc_scr[i] = jnp.zeros((DIFF_V, TILE), F32)

    def qk_meta(buf):
        row = lax.broadcasted_iota(jnp.int32, (META_KEYS, TILE), 0)
        col = lax.broadcasted_iota(jnp.int32, (META_KEYS, TILE), 1)
        limit = jnp.where(is_x, N_META - 1, jnp.minimum(col, N_META - 1))
        w_first = jnp.where(jnp.logical_and(is_x, t == 0), 1.0, 0.0).astype(F32)
        w_meta = jnp.where(is_x, 0.0, 1.0).astype(F32)
        for i in range(n_maps):
            b = w_first * bias_ref[2, i, 0:META_KEYS, :] + w_meta * bias_ref[0, i, 0:META_KEYS, :]
            st = _dot_nt(kmeta_ref[i, 0:META_KEYS, :], q_ref[i]) + b
            buf[i, 0:META_KEYS, :] = jnp.where(row <= limit, st, -jnp.inf)

    def qk_x(buf, j, kind):
        if kind == "diag":
            rowd = lax.broadcasted_iota(jnp.int32, (TILE, TILE), 0)
            cold = lax.broadcasted_iota(jnp.int32, (TILE, TILE), 1)
        for i in range(n_maps):
            st = _dot_nt(k_ref[i, j], q_ref[i])
            if kind == "diag":
                st = jnp.where(rowd <= cold, st + bias_ref[0, i], -jnp.inf)
            elif kind == "prev":
                st = st + bias_ref[1, i]
            buf[i] = st

    def update(buf, rows, vt_of):
        for i in range(n_maps):
            m_prev = m_scr[i]
            m_new = jnp.maximum(m_prev, jnp.max(buf[i, 0:rows, :], axis=0, keepdims=True))
            alpha = jnp.exp2(m_prev - m_new)
            p = jnp.exp2(buf[i, 0:rows, :] - m_new)
            l_scr[i] = alpha * l_scr[i] + jnp.sum(p, axis=0, keepdims=True)
            acc_scr[i] = alpha * acc_scr[i] + _dot(vt_of(i // 2), p.astype(BF16))
            m_scr[i] = m_new

    def update_meta(buf):
        update(buf, META_KEYS, lambda h: vtmeta_ref[h, :, 0:META_KEYS])

    def update_x(buf, j):
        update(buf, TILE, lambda h: vt_ref[h, j])

    _chunk_pipeline(is_x, t, qk_meta, qk_x, update_meta, update_x, st0_scr, st1_scr)

    lam =(jnp.exp(jnp.sum(lq1_ref[...] * lk1_ref[...], axis=-1, keepdims=True))
           - jnp.exp(jnp.sum(lq2_ref[...] * lk2_ref[...], axis=-1, keepdims=True)) + lam_init)
    for h in range(n_maps // 2):
        o_t = acc_scr[2 * h] / l_scr[2 * h] - lam * (acc_scr[2 * h + 1] / l_scr[2 * h + 1])
        o = o_t.T
        o_ref[:, h * DIFF_V:(h + 1) * DIFF_V] = (_rms(o, gsub_ref[...]) * (1.0 - lam_init)).astype(BF16)


def _oproj_body(n_x_tiles, om_ref, od_ref, wo_ref, x_ref, meta_ref, g_post_ref, g_pre_ref, h1_ref, n2_ref):
    m = pl.program_id(0)
    n_m = om_ref.shape[1]
    a = _dot(om_ref[...], wo_ref[0:n_m, :]) + _dot(od_ref[...], wo_ref[n_m:, :])
    h1_ref[...] = _rms(a, g_post_ref[...])

    @pl.when(m < n_x_tiles)
    def _():
        h1_ref[...] += x_ref[...]

    @pl.when(m == n_x_tiles)
    def _():
        h1_ref[...] += meta_ref[...]

    n2_ref[...] = _rms(h1_ref[...], g_pre_ref[...]).astype(BF16)


def _ffn_body(n_f,
              n2_ref, halo_ref, h1_ref, wg_ref, wv_ref, wd_ref, cwg_ref, cwv_ref, cbg_ref, cbv_ref, g_ref,
              out_ref, n2e_scr, up_scr):
    f = pl.program_id(1)
    tm = n2_ref.shape[0]

    @pl.when(f == 0)
    def _():
        n2e_scr[0:FFN_HALO, :] = halo_ref[...]
        n2e_scr[FFN_HALO:, :] = n2_ref[...]
        out_ref[...] = jnp.zeros(out_ref.shape, F32)

    def conv_branch(w_ref, cw_ref, cb_ref):
        up_scr[...] = _dot(n2e_scr[...], w_ref[...])
        acc = cb_ref[...] + cw_ref[0:1, :] * up_scr[FFN_HALO - 2:FFN_HALO - 2 + tm, :]
        acc = acc + cw_ref[1:2, :] * up_scr[FFN_HALO - 1:FFN_HALO - 1 + tm, :]
        return acc + cw_ref[2:3, :] * up_scr[FFN_HALO:FFN_HALO + tm, :]

    gate = conv_branch(wg_ref, cwg_ref, cbg_ref)
    val = conv_branch(wv_ref, cwv_ref, cbv_ref)
    act = (gate / (1.0 + jnp.exp(-gate)) * val).astype(BF16)
    out_ref[...] += _dot(act, wd_ref[...])

    @pl.when(f == n_f - 1)
    def _():
        out_ref[...] = h1_ref[...] + _rms(out_ref[...], g_ref[...])


def _t5_bucket_np(n):
    n = np.maximum(n, 0)
    max_exact = NUM_BUCKETS // 2
    nf = np.maximum(n, 1).astype(np.float32)
    large = max_exact + (np.log(nf / np.float32(max_exact)) / np.float32(math.log(REL_MAX_DIST / max_exact))
                         * np.float32(NUM_BUCKETS - max_exact)).astype(np.int32)
    large = np.minimum(large, NUM_BUCKETS - 1)
    return np.where(n < max_exact, n, large).astype(np.int32)


def _bias_patterns():
    kk = np.arange(TILE)[:, None]
    qq = np.arange(TILE)[None, :]
    return np.stack([_t5_bucket_np(qq - kk), _t5_bucket_np(TILE + qq - kk), _t5_bucket_np(qq + N_META - kk)])


def kernel(x, meta_tokens, rel_bias, w_in, w_uq, w_ukv, w_o, g_attn_pre, g_attn_post, g_cq, g_ckv,
           lambda_q1, lambda_k1, lambda_q2, lambda_k2, g_diff_sub, g_ffn_pre, g_ffn_post,
           w_up, conv_w, conv_b, w_down):
    B, S, D = x.shape
    assert w_in.shape[0] == 1, "single-layer problem"
    assert S % TILE == 0 and S % FFN_TM == 0
    d_ff = w_down.shape[1]
    assert d_ff % FFN_TF == 0
    tpb = S // TILE
    nxt = B * tpb
    nt = nxt + 1
    R = nt * TILE
    n_bias_maps = rel_bias.shape[1]
    lam_init = 0.8 - 0.6 * math.exp(-0.3 * 0)
    d_mix = MLA_HEADS * MLA_V + DIFF_HEADS * DIFF_V

    def rot_cols(w):
        half = w.shape[-1] // 2
        return jnp.concatenate([-w[..., half:], w[..., :half]], axis=-1)

    wi = w_in[0]
    c0 = Q_LORA + KV_LORA
    k_r = wi[:, c0:c0 + MLA_ROPE]
    w_in_p = jnp.concatenate([wi[:, :c0], wi[:, c0 + MLA_ROPE:], k_r, rot_cols(k_r)], axis=1).astype(BF16)
    wq = w_uq[0].reshape(Q_LORA, MLA_HEADS, MLA_QK)
    q_rope_w = wq[:, :, MLA_NOPE:]
    w_uq_p = jnp.concatenate([wq[:, :, :MLA_NOPE].reshape(Q_LORA, -1), q_rope_w.reshape(Q_LORA, -1),
                              rot_cols(q_rope_w).reshape(Q_LORA, -1)], axis=1).astype(BF16)
    wkv = w_ukv[0].reshape(KV_LORA, MLA_HEADS, MLA_NOPE + MLA_V)
    w_ukv_p = jnp.concatenate([wkv[:, :, :MLA_NOPE].reshape(KV_LORA, -1),
                               wkv[:, :, MLA_NOPE:].reshape(KV_LORA, -1)], axis=1).astype(BF16)
    w_o_b = w_o[0].astype(BF16)
    w_up_b = w_up[0].astype(BF16)
    w_down_b = w_down[0].astype(BF16)

    half = MLA_ROPE // 2
    inv = ROPE_THETA ** (-jnp.arange(half, dtype=F32) / half)
    pos = jnp.concatenate([jnp.arange(S, dtype=jnp.int32) + N_META, jnp.arange(TILE, dtype=jnp.int32)])
    ang = pos.astype(F32)[:, None] * inv[None, :]
    cos64 = jnp.tile(jnp.cos(ang), (1, 2))
    sin64 = jnp.tile(jnp.sin(ang), (1, 2))
    cos_q = jnp.tile(cos64, (1, MLA_HEADS))
    sin_q = jnp.tile(sin64, (1, MLA_HEADS))
    cs_k = jnp.concatenate([cos64, sin64], axis=1)

    x2 = x.reshape(B * S, D)
    meta_pad = jnp.concatenate([meta_tokens.astype(x.dtype), jnp.zeros((TILE - N_META, D), x.dtype)], axis=0)
    row1 = lambda v: v.reshape(1, -1)

    x_idx = lambda m: (jnp.minimum(m, nxt - 1), 0)
    tab_idx = lambda m: (jnp.where(m < nxt, lax.rem(m, tpb), tpb), 0)
    const2 = lambda m: (0, 0)
    d_in_p = w_in_p.shape[1]
    resident = dict(pipeline_mode=pl.Buffered(1))

    qm, km, vmt, qd, kd, vdt = pl.pallas_call(
        functools.partial(_proj_body, nxt, LOG2E / math.sqrt(MLA_QK), LOG2E / math.sqrt(DIFF_DK)),
        grid=(nt,),
        in_specs=[
            pl.BlockSpec((TILE, D), x_idx),
            pl.BlockSpec((TILE, D), const2),
            pl.BlockSpec((TILE, MLA_HEADS * MLA_ROPE), tab_idx),
            pl.BlockSpec((TILE, MLA_HEADS * MLA_ROPE), tab_idx),
            pl.BlockSpec((TILE, 2 * MLA_ROPE), tab_idx),
            pl.BlockSpec((1, D), const2),
            pl.BlockSpec((D, d_in_p), const2, **resident),
            pl.BlockSpec((1, Q_LORA), const2),
            pl.BlockSpec((1, KV_LORA), const2),
            pl.BlockSpec(w_uq_p.shape, const2, **resident),
            pl.BlockSpec(w_ukv_p.shape, const2, **resident),
        ],
        out_specs=[
            pl.BlockSpec((MLA_HEADS, TILE, MLA_QK), lambda m: (0, m, 0)),
            pl.BlockSpec((MLA_HEADS, 1, TILE, MLA_QK), lambda m: (0, m, 0, 0)),
            pl.BlockSpec((MLA_HEADS, 1, MLA_V, TILE), lambda m: (0, m, 0, 0)),
            pl.BlockSpec((2 * DIFF_HEADS, TILE, DIFF_DK), lambda m: (0, m, 0)),
            pl.BlockSpec((2 * DIFF_HEADS, 1, TILE, DIFF_DK), lambda m: (0, m, 0, 0)),
            pl.BlockSpec((DIFF_HEADS, 1, DIFF_V, TILE), lambda m: (0, m, 0, 0)),
        ],
        out_shape=[
            jax.ShapeDtypeStruct((MLA_HEADS, R, MLA_QK), BF16),
            jax.ShapeDtypeStruct((MLA_HEADS, nt, TILE, MLA_QK), BF16),
            jax.ShapeDtypeStruct((MLA_HEADS, nt, MLA_V, TILE), BF16),
            jax.ShapeDtypeStruct((2 * DIFF_HEADS, R, DIFF_DK), BF16),
            jax.ShapeDtypeStruct((2 * DIFF_HEADS, nt, TILE, DIFF_DK), BF16),
            jax.ShapeDtypeStruct((DIFF_HEADS, nt, DIFF_V, TILE), BF16),
        ],
        scratch_shapes=[
            pltpu.VMEM((TILE, D), BF16),
            pltpu.VMEM((TILE, d_in_p), F32),
            pltpu.VMEM((TILE, w_uq_p.shape[1]), F32),
            pltpu.VMEM((TILE, w_ukv_p.shape[1]), F32),
        ],
        compiler_params=pltpu.CompilerParams(dimension_semantics=("arbitrary",), vmem_limit_bytes=VMEM_LIMIT),
        name="in_proj",
    )(x2, meta_pad, cos_q, sin_q, cs_k, row1(g_attn_pre[0]), w_in_p, row1(g_cq[0]), row1(g_ckv[0]),
      w_uq_p, w_ukv_p)

    batch_of = lambda s: jnp.minimum(s // tpb, B - 1)

    gm = MLA_GROUP
    o_m = pl.pallas_call(
        functools.partial(_mla_body, nxt, tpb),
        grid=(MLA_HEADS // gm, nt),
        in_specs=[
            pl.BlockSpec((gm, TILE, MLA_QK), lambda h, s: (h, s, 0)),
            pl.BlockSpec((gm, tpb, TILE, MLA_QK), lambda h, s: (h, batch_of(s), 0, 0)),
            pl.BlockSpec((gm, tpb, MLA_V, TILE), lambda h, s: (h, batch_of(s), 0, 0)),
            pl.BlockSpec((gm, None, TILE, MLA_QK), lambda h, s: (h, nxt, 0, 0)),
            pl.BlockSpec((gm, None, MLA_V, TILE), lambda h, s: (h, nxt, 0, 0)),
        ],
        out_specs=pl.BlockSpec((TILE, gm * MLA_V), lambda h, s: (s, h)),
        out_shape=jax.ShapeDtypeStruct((R, MLA_HEADS * MLA_V), BF16),
        scratch_shapes=[
            pltpu.VMEM((gm, 1, TILE), F32),
            pltpu.VMEM((gm, 1, TILE), F32),
            pltpu.VMEM((gm, MLA_V, TILE), F32),
            pltpu.VMEM((gm, TILE, TILE), F32),
            pltpu.VMEM((gm, TILE, TILE), F32),
        ],
        compiler_params=pltpu.CompilerParams(dimension_semantics=("arbitrary", "arbitrary"),
                                             vmem_limit_bytes=VMEM_LIMIT),
        name="mla_attn",
    )(qm, km, vmt, km, vmt)

    patterns = jnp.asarray(_bias_patterns())
    bias_tiles = pl.pallas_call(
        _bias_body,
        grid=(patterns.shape[0],),
        in_specs=[
            pl.BlockSpec((1, TILE, TILE), lambda p: (p, 0, 0)),
            pl.BlockSpec(memory_space=pltpu.SMEM),
        ],
        out_specs=pl.BlockSpec((1, n_bias_maps, TILE, TILE), lambda p: (p, 0, 0, 0)),
        out_shape=jax.ShapeDtypeStruct((patterns.shape[0], n_bias_maps, TILE, TILE), F32),
        compiler_params=pltpu.CompilerParams(dimension_semantics=("arbitrary",)),
        name="rel_bias_tiles",
    )(patterns, rel_bias.astype(F32))

    gd = DIFF_GROUP
    o_d = pl.pallas_call(
        functools.partial(_diff_body, nxt, tpb, lam_init),
        grid=(DIFF_HEADS // gd, nt),
        in_specs=[
            pl.BlockSpec((2 * gd, TILE, DIFF_DK), lambda h, s: (h, s, 0)),
            pl.BlockSpec((2 * gd, tpb, TILE, DIFF_DK), lambda h, s: (h, batch_of(s), 0, 0)),
            pl.BlockSpec((gd, tpb, DIFF_V, TILE), lambda h, s: (h, batch_of(s), 0, 0)),
            pl.BlockSpec((2 * gd, None, TILE, DIFF_DK), lambda h, s: (h, nxt, 0, 0)),
            pl.BlockSpec((gd, None, DIFF_V, TILE), lambda h, s: (h, nxt, 0, 0)),
            pl.BlockSpec((patterns.shape[0], 2 * gd, TILE, TILE), lambda h, s: (0, h, 0, 0)),
            pl.BlockSpec((1, DIFF_DK), lambda h, s: (0, 0)),
            pl.BlockSpec((1, DIFF_DK), lambda h, s: (0, 0)),
            pl.BlockSpec((1, DIFF_DK), lambda h, s: (0, 0)),
            pl.BlockSpec((1, DIFF_DK), lambda h, s: (0, 0)),
            pl.BlockSpec((1, DIFF_V), lambda h, s: (0, 0)),
        ],
        out_specs=pl.BlockSpec((TILE, gd * DIFF_V), lambda h, s: (s, h)),
        out_shape=jax.ShapeDtypeStruct((R, DIFF_HEADS * DIFF_V), BF16),
        scratch_shapes=[
            pltpu.VMEM((2 * gd, 1, TILE), F32),
            pltpu.VMEM((2 * gd, 1, TILE), F32),
            pltpu.VMEM((2 * gd, DIFF_V, TILE), F32),
            pltpu.VMEM((2 * gd, TILE, TILE), F32),
            pltpu.VMEM((2 * gd, TILE, TILE), F32),
        ],
        compiler_params=pltpu.CompilerParams(dimension_semantics=("arbitrary", "arbitrary"),
                                             vmem_limit_bytes=VMEM_LIMIT),
        name="diff_attn",
    )(qd, kd, vdt, kd, vdt, bias_tiles, lambda_q1.astype(F32), lambda_k1.astype(F32),
      lambda_q2.astype(F32), lambda_k2.astype(F32), row1(g_diff_sub[0]))

    h1, n2 = pl.pallas_call(
        functools.partial(_oproj_body, nxt),
        grid=(nt,),
        in_specs=[
            pl.BlockSpec((TILE, MLA_HEADS * MLA_V), lambda m: (m, 0)),
            pl.BlockSpec((TILE, DIFF_HEADS * DIFF_V), lambda m: (m, 0)),
            pl.BlockSpec((d_mix, D), const2, **resident),
            pl.BlockSpec((TILE, D), x_idx),
            pl.BlockSpec((TILE, D), const2),
            pl.BlockSpec((1, D), const2),
            pl.BlockSpec((1, D), const2),
        ],
        out_specs=[
            pl.BlockSpec((TILE, D), lambda m: (m, 0)),
            pl.BlockSpec((TILE, D), lambda m: (m, 0)),
        ],
        out_shape=[
            jax.ShapeDtypeStruct((R, D), F32),
            jax.ShapeDtypeStruct((R, D), BF16),
        ],
        compiler_params=pltpu.CompilerParams(dimension_semantics=("arbitrary",), vmem_limit_bytes=VMEM_LIMIT),
        name="out_proj",
    )(o_m, o_d, w_o_b, x2, meta_pad, row1(g_attn_post[0]), row1(g_ffn_pre[0]))

    n_m = (B * S) // FFN_TM
    n_f = d_ff // FFN_TF
    tiles_per_batch_ffn = S // FFN_TM
    halo_per_tile = FFN_TM // FFN_HALO
    meta_halo_blk = (B * S) // FFN_HALO

    def halo_idx(m, f):
        return (jnp.where(lax.rem(m, tiles_per_batch_ffn) == 0, meta_halo_blk, m * halo_per_tile - 1), 0)

    out = pl.pallas_call(
        functools.partial(_ffn_body, n_f),
        grid=(n_m, n_f),
        in_specs=[
            pl.BlockSpec((FFN_TM, D), lambda m, f: (m, 0)),
            pl.BlockSpec((FFN_HALO, D), halo_idx),
            pl.BlockSpec((FFN_TM, D), lambda m, f: (m, 0)),
            pl.BlockSpec((D, FFN_TF), lambda m, f: (0, f)),
            pl.BlockSpec((D, FFN_TF), lambda m, f: (0, n_f + f)),
            pl.BlockSpec((FFN_TF, D), lambda m, f: (f, 0)),
            pl.BlockSpec((CONV_W, FFN_TF), lambda m, f: (0, f)),
            pl.BlockSpec((CONV_W, FFN_TF), lambda m, f: (0, n_f + f)),
            pl.BlockSpec((1, FFN_TF), lambda m, f: (0, f)),
            pl.BlockSpec((1, FFN_TF), lambda m, f: (0, n_f + f)),
            pl.BlockSpec((1, D), lambda m, f: (0, 0)),
        ],
        out_specs=pl.BlockSpec((FFN_TM, D), lambda m, f: (m, 0)),
        out_shape=jax.ShapeDtypeStruct((B * S, D), x.dtype),
        scratch_shapes=[
            pltpu.VMEM((FFN_HALO + FFN_TM, D), BF16),
            pltpu.VMEM((FFN_HALO + FFN_TM, FFN_TF), F32),
        ],
        compiler_params=pltpu.CompilerParams(dimension_semantics=("arbitrary", "arbitrary"),
                                             vmem_limit_bytes=VMEM_LIMIT),
        name="conv_ffn",
    )(n2, n2, h1, w_up_b, w_up_b, w_down_b, conv_w[0], conv_w[0], row1(conv_b[0]), row1(conv_b[0]),
      row1(g_ffn_post[0]))

    return out.reshape(B, S, D)
```

```python
import functools
import math

import numpy as np
import jax
import jax.numpy as jnp
from jax import lax
from jax.experimental import pallas as pl
from jax.experimental.pallas import tpu as pltpu

F32 = jnp.float32
BF16 = jnp.bfloat16

N_META = 16
MLA_HEADS = 8
MLA_NOPE = 128
MLA_ROPE = 64
MLA_V = 128
MLA_QK = MLA_NOPE + MLA_ROPE
Q_LORA = 512
KV_LORA = 256
ROPE_THETA = 10000.0
DIFF_HEADS = 4
DIFF_DK = 128
DIFF_V = 2 * DIFF_DK
DIFF_QK = DIFF_HEADS * 2 * DIFF_DK
NUM_BUCKETS = 32
REL_MAX_DIST = 128
CONV_W = 3
EPS = 1e-6
LOG2E = math.log2(math.e)

TILE = 256
META_KEYS = 128
ONES_ROWS = 16
MLA_GROUP = 4
DIFF_GROUP = 2
FFN_TM = 512
FFN_TF = 512
FFN_HALO = 16
VMEM_LIMIT = 56 * 1024 * 1024


def _rms(x, g):
    ms = jnp.mean(x * x, axis=-1, keepdims=True)
    return x * lax.rsqrt(ms + EPS) * g


def _dot(a, b):
    return jnp.dot(a, b, preferred_element_type=F32)


def _dot_nt(a, b):
    return lax.dot_general(a, b, (((1,), (1,)), ((), ())), preferred_element_type=F32)


def _proj_body(n_x_tiles, q_scale_mla, q_scale_diff,
               x_ref, meta_ref, cos_ref, sin_ref, cs_ref, g_pre_ref, w_in_ref, g_cq_ref, g_ckv_ref,
               w_uq_ref, w_ukv_ref,
               qm_ref, km_ref, vmt_ref, qd_ref, kd_ref, vdt_ref,
               n_scr, proj_scr, qm_scr, kv_scr):
    m = pl.program_id(0)

    @pl.when(m < n_x_tiles)
    def _():
        n_scr[...] = _rms(x_ref[...], g_pre_ref[...]).astype(BF16)

    @pl.when(m == n_x_tiles)
    def _():
        n_scr[...] = _rms(meta_ref[...], g_pre_ref[...]).astype(BF16)

    proj_scr[...] = _dot(n_scr[...], w_in_ref[...])
    c_q = _rms(proj_scr[:, 0:Q_LORA], g_cq_ref[...]).astype(BF16)
    c_kv = _rms(proj_scr[:, Q_LORA:Q_LORA + KV_LORA], g_ckv_ref[...]).astype(BF16)
    qm_scr[...] = _dot(c_q, w_uq_ref[...])
    kv_scr[...] = _dot(c_kv, w_ukv_ref[...])

    n_nope = MLA_HEADS * MLA_NOPE
    n_rope = MLA_HEADS * MLA_ROPE
    q_rope = (qm_scr[:, n_nope:n_nope + n_rope] * cos_ref[...]
              + qm_scr[:, n_nope + n_rope:n_nope + 2 * n_rope] * sin_ref[...])
    off_kr = Q_LORA + KV_LORA + 2 * DIFF_QK + DIFF_HEADS * DIFF_V
    kt = proj_scr[:, off_kr:off_kr + 2 * MLA_ROPE] * cs_ref[...]
    k_rope = (kt[:, 0:MLA_ROPE] + kt[:, MLA_ROPE:2 * MLA_ROPE]).astype(BF16)

    for h in range(MLA_HEADS):
        qm_ref[h, :, 0:MLA_NOPE] = (qm_scr[:, h * MLA_NOPE:(h + 1) * MLA_NOPE] * q_scale_mla).astype(BF16)
        qm_ref[h, :, MLA_NOPE:MLA_QK] = (q_rope[:, h * MLA_ROPE:(h + 1) * MLA_ROPE] * q_scale_mla).astype(BF16)
        km_ref[h, 0, :, 0:MLA_NOPE] = kv_scr[:, h * MLA_NOPE:(h + 1) * MLA_NOPE].astype(BF16)
        km_ref[h, 0, :, MLA_NOPE:MLA_QK] = k_rope
        v = kv_scr[:, n_nope + h * MLA_V:n_nope + (h + 1) * MLA_V]
        vmt_ref[h, 0, 0:MLA_V, :] = v.T.astype(BF16)
        vmt_ref[h, 0, MLA_V:MLA_V + ONES_ROWS, :] = jnp.ones((ONES_ROWS, TILE), BF16)

    off_qd = Q_LORA + KV_LORA
    off_kd = off_qd + DIFF_QK
    off_vd = off_kd + DIFF_QK
    for i in range(2 * DIFF_HEADS):
        qd_ref[i] = (proj_scr[:, off_qd + i * DIFF_DK:off_qd + (i + 1) * DIFF_DK] * q_scale_diff).astype(BF16)
        kd_ref[i, 0] = proj_scr[:, off_kd + i * DIFF_DK:off_kd + (i + 1) * DIFF_DK].astype(BF16)
    for h in range(DIFF_HEADS):
        v = proj_scr[:, off_vd + h * DIFF_V:off_vd + (h + 1) * DIFF_V]
        vdt_ref[h, 0, 0:DIFF_V, :] = v.T.astype(BF16)
        vdt_ref[h, 0, DIFF_V:DIFF_V + ONES_ROWS, :] = jnp.ones((ONES_ROWS, TILE), BF16)


def _chunk_pipeline(is_x, t, qk_meta, qk_x, update_meta, update_x, b0, b1):
    qk_meta(b0)

    @pl.when(jnp.logical_not(is_x))
    def _():
        update_meta(b0)

    @pl.when(is_x)
    def _():
        qk_x(b1, t, "diag")
        update_meta(b0)

        @pl.when(t == 0)
        def _():
            update_x(b1, t)

        @pl.when(t >= 1)
        def _():
            qk_x(b0, t - 1, "prev")
            update_x(b1, t)
            n_far = t - 1
            n_pairs = n_far // 2

            def pair(k, carry):
                qk_x(b1, 2 * k, "far")
                update_x(b0, jnp.where(k == 0, t - 1, 2 * k - 1))
                qk_x(b0, 2 * k + 1, "far")
                update_x(b1, 2 * k)
                return carry

            lax.fori_loop(0, n_pairs, pair, 0)
            j_pending = jnp.where(n_pairs == 0, t - 1, 2 * n_pairs - 1)

            @pl.when(n_far % 2 == 1)
            def _():
                qk_x(b1, n_far - 1, "far")
                update_x(b0, j_pending)
                update_x(b1, n_far - 1)

            @pl.when(n_far % 2 == 0)
            def _():
                update_x(b0, j_pending)


def _mla_body(n_x_tiles, tiles_per_batch,
              q_ref, k_ref, vt_ref, kmeta_ref, vtmeta_ref, o_ref,
              m_scr, acc_scr, st0_scr, mx0_scr, st1_scr, mx1_scr):
    n_heads = q_ref.shape[0]
    s = pl.program_id(1)
    is_x = s < n_x_tiles
    t = lax.rem(s, tiles_per_batch)

    for g in range(n_heads):
        m_scr[g] = jnp.full((1, TILE), -jnp.inf, F32)
        acc_scr[g] = jnp.zeros((MLA_V + ONES_ROWS, TILE), F32)

    def qk_meta(buf):
        st_buf, mx_buf = buf
        row = lax.broadcasted_iota(jnp.int32, (META_KEYS, TILE), 0)
        col = lax.broadcasted_iota(jnp.int32, (META_KEYS, TILE), 1)
        limit = jnp.where(is_x, N_META - 1, jnp.minimum(col, N_META - 1))
        for g in range(n_heads):
            st = jnp.where(row <= limit, _dot_nt(kmeta_ref[g, 0:META_KEYS, :], q_ref[g]), -jnp.inf)
            st_buf[g, 0:META_KEYS, :] = st
            mx_buf[g] = jnp.max(st, axis=0, keepdims=True)

    def qk_x(buf, j, kind):
        st_buf, mx_buf = buf
        if kind == "diag":
            rowd = lax.broadcasted_iota(jnp.int32, (TILE, TILE), 0)
            cold = lax.broadcasted_iota(jnp.int32, (TILE, TILE), 1)
        for g in range(n_heads):
            st = _dot_nt(k_ref[g, j], q_ref[g])
            if kind == "diag":
                st = jnp.where(rowd <= cold, st, -jnp.inf)
            st_buf[g] = st
            mx_buf[g] = jnp.max(st, axis=0, keepdims=True)

    def update(buf, rows, vt_of):
        st_buf, mx_buf = buf
        for g in range(n_heads):
            m_prev = m_scr[g]
            m_new = jnp.maximum(m_prev, mx_buf[g])
            alpha = jnp.exp2(m_prev - m_new)
            p = jnp.exp2(st_buf[g, 0:rows, :] - m_new)
            acc_scr[g] = alpha * acc_scr[g] + _dot(vt_of(g), p.astype(BF16))
            m_scr[g] = m_new

    def update_meta(buf):
        update(buf, META_KEYS, lambda g: vtmeta_ref[g, :, 0:META_KEYS])

    def update_x(buf, j):
        update(buf, TILE, lambda g: vt_ref[g, j])

    _chunk_pipeline(is_x, t, qk_meta, qk_x, update_meta, update_x, (st0_scr, mx0_scr), (st1_scr, mx1_scr))

    for g in range(n_heads):
        o_t = acc_scr[g, 0:MLA_V, :] / acc_scr[g, MLA_V:MLA_V + 1, :]
        o_ref[:, g * MLA_V:(g + 1) * MLA_V] = o_t.T.astype(BF16)


def _bias_body(idx_ref, rel_ref, out_ref):
    idx = idx_ref[0]
    n_maps = out_ref.shape[1]
    for mp in range(n_maps):
        far = jnp.full(idx.shape, rel_ref[NUM_BUCKETS - 1, mp], F32)
        acc = far
        for b in range(NUM_BUCKETS - 1):
            acc = jnp.where(idx == b, rel_ref[b, mp], acc)
        out_ref[0, mp] = (acc - far) * LOG2E


def _diff_body(n_x_tiles, tiles_per_batch, lam_init,
               q_ref, k_ref, vt_ref, kmeta_ref, vtmeta_ref, bias_ref,
               lq1_ref, lk1_ref, lq2_ref, lk2_ref, gsub_ref, o_ref,
               m_scr, acc_scr, st0_scr, mx0_scr, st1_scr, mx1_scr):
    n_maps = q_ref.shape[0]
    s = pl.program_id(1)
    is_x = s < n_x_tiles
    t = lax.rem(s, tiles_per_batch)

    for i in range(n_maps):
        m_scr[i] = jnp.full((1, TILE), -jnp.inf, F32)
        acc_scr[i] = jnp.zeros((DIFF_V + ONES_ROWS, TILE), F32)

    def qk_meta(buf):
        st_buf, mx_buf = buf
        row = lax.broadcasted_iota(jnp.int32, (META_KEYS, TILE), 0)
        col = lax.broadcasted_iota(jnp.int32, (META_KEYS, TILE), 1)
        limit = jnp.where(is_x, N_META - 1, jnp.minimum(col, N_META - 1))
        w_first = jnp.where(jnp.logical_and(is_x, t == 0), 1.0, 0.0).astype(F32)
        w_meta = jnp.where(is_x, 0.0, 1.0).astype(F32)
        for i in range(n_maps):
            b = w_first * bias_ref[2, i, 0:META_KEYS, :] + w_meta * bias_ref[0, i, 0:META_KEYS, :]
            st = jnp.where(row <= limit, _dot_nt(kmeta_ref[i, 0:META_KEYS, :], q_ref[i]) + b, -jnp.inf)
            st_buf[i, 0:META_KEYS, :] = st
            mx_buf[i] = jnp.max(st, axis=0, keepdims=True)

    def qk_x(buf, j, kind):
        st_buf, mx_buf = buf
        if kind == "diag":
            rowd = lax.broadcasted_iota(jnp.int32, (TILE, TILE), 0)
            cold = lax.broadcasted_iota(jnp.int32, (TILE, TILE), 1)
        for i in range(n_maps):
            st = _dot_nt(k_ref[i, j], q_ref[i])
            if kind == "diag":
                st = jnp.where(rowd <= cold, st + bias_ref[0, i], -jnp.inf)
            elif kind == "prev":
                st = st + bias_ref[1, i]
            st_buf[i] = st
            mx_buf[i] = jnp.max(st, axis=0, keepdims=True)

    def update(buf, rows, vt_of):
        st_buf, mx_buf = buf
        for i in range(n_maps):
            m_prev = m_scr[i]
            m_new = jnp.maximum(m_prev, mx_buf[i])
            alpha = jnp.exp2(m_prev - m_new)
            p = jnp.exp2(st_buf[i, 0:rows, :] - m_new)
            acc_scr[i] = alpha * acc_scr[i] + _dot(vt_of(i // 2), p.astype(BF16))
            m_scr[i] = m_new

    def update_meta(buf):
        update(buf, META_KEYS, lambda h: vtmeta_ref[h, :, 0:META_KEYS])

    def update_x(buf, j):
        update(buf, TILE, lambda h: vt_ref[h, j])

    _chunk_pipeline(is_x, t, qk_meta, qk_x, update_meta, update_x, (st0_scr, mx0_scr), (st1_scr, mx1_scr))

    lam = (jnp.exp(jnp.sum(lq1_ref[...] * lk1_ref[...], axis=-1, keepdims=True))
           - jnp.exp(jnp.sum(lq2_ref[...] * lk2_ref[...], axis=-1, keepdims=True)) + lam_init)
    for h in range(n_maps // 2):
        a0 = acc_scr[2 * h, 0:DIFF_V, :] / acc_scr[2 * h, DIFF_V:DIFF_V + 1, :]
        a1 = acc_scr[2 * h + 1, 0:DIFF_V, :] / acc_scr[2 * h + 1, DIFF_V:DIFF_V + 1, :]
        o_t = a0 - lam * a1
        o = o_t.T
        o_ref[:, h * DIFF_V:(h + 1) * DIFF_V] = (_rms(o, gsub_ref[...]) * (1.0 - lam_init)).astype(BF16)


def _oproj_body(n_x_tiles, om_ref, od_ref, wo_ref, x_ref, meta_ref, g_post_ref, g_pre_ref, h1_ref, n2_ref):
    m = pl.program_id(0)
    n_m = om_ref.shape[1]
    a = _dot(om_ref[...], wo_ref[0:n_m, :]) + _dot(od_ref[...], wo_ref[n_m:, :])
    h1_ref[...] = _rms(a, g_post_ref[...])

    @pl.when(m < n_x_tiles)
    def _():
        h1_ref[...] += x_ref[...]

    @pl.when(m == n_x_tiles)
    def _():
        h1_ref[...] += meta_ref[...]

    n2_ref[...] = _rms(h1_ref[...], g_pre_ref[...]).astype(BF16)


def _ffn_body(n_f,
              n2_ref, halo_ref, h1_ref, wg_ref, wv_ref, wd_ref, cwg_ref, cwv_ref, cbg_ref, cbv_ref, g_ref,
              out_ref, n2e_scr, up_scr):
    f = pl.program_id(1)
    tm = n2_ref.shape[0]

    @pl.when(f == 0)
    def _():
        n2e_scr[0:FFN_HALO, :] = halo_ref[...]
        n2e_scr[FFN_HALO:, :] = n2_ref[...]
        out_ref[...] = jnp.zeros(out_ref.shape, F32)

    def conv_branch(w_ref, cw_ref, cb_ref):
        up_scr[...] = _dot(n2e_scr[...], w_ref[...])
        acc = cb_ref[...] + cw_ref[0:1, :] * up_scr[FFN_HALO - 2:FFN_HALO - 2 + tm, :]
        acc = acc + cw_ref[1:2, :] * up_scr[FFN_HALO - 1:FFN_HALO - 1 + tm, :]
        return acc + cw_ref[2:3, :] * up_scr[FFN_HALO:FFN_HALO + tm, :]

    gate = conv_branch(wg_ref, cwg_ref, cbg_ref)
    val = conv_branch(wv_ref, cwv_ref, cbv_ref)
    act = (gate / (1.0 + jnp.exp(-gate)) * val).astype(BF16)
    out_ref[...] += _dot(act, wd_ref[...])

    @pl.when(f == n_f - 1)
    def _():
        out_ref[...] = h1_ref[...] + _rms(out_ref[...], g_ref[...])


def _t5_bucket_np(n):
    n = np.maximum(n, 0)
    max_exact = NUM_BUCKETS // 2
    nf = np.maximum(n, 1).astype(np.float32)
    large = max_exact + (np.log(nf / np.float32(max_exact)) / np.float32(math.log(REL_MAX_DIST / max_exact))
                         * np.float32(NUM_BUCKETS - max_exact)).astype(np.int32)
    large = np.minimum(large, NUM_BUCKETS - 1)
    return np.where(n < max_exact, n, large).astype(np.int32)


def _bias_patterns():
    kk = np.arange(TILE)[:, None]
    qq = np.arange(TILE)[None, :]
    return np.stack([_t5_bucket_np(qq - kk), _t5_bucket_np(TILE + qq - kk), _t5_bucket_np(qq + N_META - kk)])


def kernel(x, meta_tokens, rel_bias, w_in, w_uq, w_ukv, w_o, g_attn_pre, g_attn_post, g_cq, g_ckv,
           lambda_q1, lambda_k1, lambda_q2, lambda_k2, g_diff_sub, g_ffn_pre, g_ffn_post,
           w_up, conv_w, conv_b, w_down):
    B, S, D = x.shape
    assert w_in.shape[0] == 1, "single-layer problem"
    assert S % TILE == 0 and S % FFN_TM == 0
    d_ff = w_down.shape[1]
    assert d_ff % FFN_TF == 0
    tpb = S // TILE
    nxt = B * tpb
    nt = nxt + 1
    R = nt * TILE
    n_bias_maps = rel_bias.shape[1]
    lam_init = 0.8 - 0.6 * math.exp(-0.3 * 0)
    d_mix = MLA_HEADS * MLA_V + DIFF_HEADS * DIFF_V
    mv_rows = MLA_V + ONES_ROWS
    dv_rows = DIFF_V + ONES_ROWS

    def rot_cols(w):
        half = w.shape[-1] // 2
        return jnp.concatenate([-w[..., half:], w[..., :half]], axis=-1)

    wi = w_in[0]
    c0 = Q_LORA + KV_LORA
    k_r = wi[:, c0:c0 + MLA_ROPE]
    w_in_p = jnp.concatenate([wi[:, :c0], wi[:, c0 + MLA_ROPE:], k_r, rot_cols(k_r)], axis=1).astype(BF16)
    wq = w_uq[0].reshape(Q_LORA, MLA_HEADS, MLA_QK)
    q_rope_w = wq[:, :, MLA_NOPE:]
    w_uq_p = jnp.concatenate([wq[:, :, :MLA_NOPE].reshape(Q_LORA, -1), q_rope_w.reshape(Q_LORA, -1),
                              rot_cols(q_rope_w).reshape(Q_LORA, -1)], axis=1).astype(BF16)
    wkv = w_ukv[0].reshape(KV_LORA, MLA_HEADS, MLA_NOPE + MLA_V)
    w_ukv_p = jnp.concatenate([wkv[:, :, :MLA_NOPE].reshape(KV_LORA, -1),
                               wkv[:, :, MLA_NOPE:].reshape(KV_LORA, -1)], axis=1).astype(BF16)
    w_o_b = w_o[0].astype(BF16)
    w_up_b = w_up[0].astype(BF16)
    w_down_b = w_down[0].astype(BF16)

    half = MLA_ROPE // 2
    inv = ROPE_THETA ** (-jnp.arange(half, dtype=F32) / half)
    pos = jnp.concatenate([jnp.arange(S, dtype=jnp.int32) + N_META, jnp.arange(TILE, dtype=jnp.int32)])
    ang = pos.astype(F32)[:, None] * inv[None, :]
    cos64 = jnp.tile(jnp.cos(ang), (1, 2))
    sin64 = jnp.tile(jnp.sin(ang), (1, 2))
    cos_q = jnp.tile(cos64, (1, MLA_HEADS))
    sin_q = jnp.tile(sin64, (1, MLA_HEADS))
    cs_k = jnp.concatenate([cos64, sin64], axis=1)

    x2 = x.reshape(B * S, D)
    meta_pad = jnp.concatenate([meta_tokens.astype(x.dtype), jnp.zeros((TILE - N_META, D), x.dtype)], axis=0)
    row1 = lambda v: v.reshape(1, -1)

    x_idx = lambda m: (jnp.minimum(m, nxt - 1), 0)
    tab_idx = lambda m: (jnp.where(m < nxt, lax.rem(m, tpb), tpb), 0)
    const2 = lambda m: (0, 0)
    d_in_p = w_in_p.shape[1]
    resident = dict(pipeline_mode=pl.Buffered(1))

    qm, km, vmt, qd, kd, vdt = pl.pallas_call(
        functools.partial(_proj_body, nxt, LOG2E / math.sqrt(MLA_QK), LOG2E / math.sqrt(DIFF_DK)),
        grid=(nt,),
        in_specs=[
            pl.BlockSpec((TILE, D), x_idx),
            pl.BlockSpec((TILE, D), const2),
            pl.BlockSpec((TILE, MLA_HEADS * MLA_ROPE), tab_idx),
            pl.BlockSpec((TILE, MLA_HEADS * MLA_ROPE), tab_idx),
            pl.BlockSpec((TILE, 2 * MLA_ROPE), tab_idx),
            pl.BlockSpec((1, D), const2),
            pl.BlockSpec((D, d_in_p), const2, **resident),
            pl.BlockSpec((1, Q_LORA), const2),
            pl.BlockSpec((1, KV_LORA), const2),
            pl.BlockSpec(w_uq_p.shape, const2, **resident),
            pl.BlockSpec(w_ukv_p.shape, const2, **resident),
        ],
        out_specs=[
            pl.BlockSpec((MLA_HEADS, TILE, MLA_QK), lambda m: (0, m, 0)),
            pl.BlockSpec((MLA_HEADS, 1, TILE, MLA_QK), lambda m: (0, m, 0, 0)),
            pl.BlockSpec((MLA_HEADS, 1, mv_rows, TILE), lambda m: (0, m, 0, 0)),
            pl.BlockSpec((2 * DIFF_HEADS, TILE, DIFF_DK), lambda m: (0, m, 0)),
            pl.BlockSpec((2 * DIFF_HEADS, 1, TILE, DIFF_DK), lambda m: (0, m, 0, 0)),
            pl.BlockSpec((DIFF_HEADS, 1, dv_rows, TILE), lambda m: (0, m, 0, 0)),
        ],
        out_shape=[
            jax.ShapeDtypeStruct((MLA_HEADS, R, MLA_QK), BF16),
            jax.ShapeDtypeStruct((MLA_HEADS, nt, TILE, MLA_QK), BF16),
            jax.ShapeDtypeStruct((MLA_HEADS, nt, mv_rows, TILE), BF16),
            jax.ShapeDtypeStruct((2 * DIFF_HEADS, R, DIFF_DK), BF16),
            jax.ShapeDtypeStruct((2 * DIFF_HEADS, nt, TILE, DIFF_DK), BF16),
            jax.ShapeDtypeStruct((DIFF_HEADS, nt, dv_rows, TILE), BF16),
        ],
        scratch_shapes=[
            pltpu.VMEM((TILE, D), BF16),
            pltpu.VMEM((TILE, d_in_p), F32),
            pltpu.VMEM((TILE, w_uq_p.shape[1]), F32),
            pltpu.VMEM((TILE, w_ukv_p.shape[1]), F32),
        ],
        compiler_params=pltpu.CompilerParams(dimension_semantics=("arbitrary",), vmem_limit_bytes=VMEM_LIMIT),
        name="in_proj",
    )(x2, meta_pad, cos_q, sin_q, cs_k, row1(g_attn_pre[0]), w_in_p, row1(g_cq[0]), row1(g_ckv[0]),
      w_uq_p, w_ukv_p)

    batch_of = lambda s: jnp.minimum(s // tpb, B - 1)

    gm = MLA_GROUP
    o_m = pl.pallas_call(
        functools.partial(_mla_body, nxt, tpb),
        grid=(MLA_HEADS // gm, nt),
        in_specs=[
            pl.BlockSpec((gm, TILE, MLA_QK), lambda h, s: (h, s, 0)),
            pl.BlockSpec((gm, tpb, TILE, MLA_QK), lambda h, s: (h, batch_of(s), 0, 0)),
            pl.BlockSpec((gm, tpb, mv_rows, TILE), lambda h, s: (h, batch_of(s), 0, 0)),
            pl.BlockSpec((gm, None, TILE, MLA_QK), lambda h, s: (h, nxt, 0, 0)),
            pl.BlockSpec((gm, None, mv_rows, TILE), lambda h, s: (h, nxt, 0, 0)),
        ],
        out_specs=pl.BlockSpec((TILE, gm * MLA_V), lambda h, s: (s, h)),
        out_shape=jax.ShapeDtypeStruct((R, MLA_HEADS * MLA_V), BF16),
        scratch_shapes=[
            pltpu.VMEM((gm, 1, TILE), F32),
            pltpu.VMEM((gm, mv_rows, TILE), F32),
            pltpu.VMEM((gm, TILE, TILE), F32),
            pltpu.VMEM((gm, 1, TILE), F32),
            pltpu.VMEM((gm, TILE, TILE), F32),
            pltpu.VMEM((gm, 1, TILE), F32),
        ],
        compiler_params=pltpu.CompilerParams(dimension_semantics=("arbitrary", "arbitrary"),
                                             vmem_limit_bytes=VMEM_LIMIT),
        name="mla_attn",
    )(qm, km, vmt, km, vmt)

    patterns = jnp.asarray(_bias_patterns())
    bias_tiles = pl.pallas_call(
        _bias_body,
        grid=(patterns.shape[0],),
        in_specs=[
            pl.BlockSpec((1, TILE, TILE), lambda p: (p, 0, 0)),
            pl.BlockSpec(memory_space=pltpu.SMEM),
        ],
        out_specs=pl.BlockSpec((1, n_bias_maps, TILE, TILE), lambda p: (p, 0, 0, 0)),
        out_shape=jax.ShapeDtypeStruct((patterns.shape[0], n_bias_maps, TILE, TILE), F32),
        compiler_params=pltpu.CompilerParams(dimension_semantics=("arbitrary",)),
        name="rel_bias_tiles",
    )(patterns, rel_bias.astype(F32))

    gd = DIFF_GROUP
    o_d = pl.pallas_call(
        functools.partial(_diff_body, nxt, tpb, lam_init),
        grid=(DIFF_HEADS // gd, nt),
        in_specs=[
            pl.BlockSpec((2 * gd, TILE, DIFF_DK), lambda h, s: (h, s, 0)),
            pl.BlockSpec((2 * gd, tpb, TILE, DIFF_DK), lambda h, s: (h, batch_of(s), 0, 0)),
            pl.BlockSpec((gd, tpb, dv_rows, TILE), lambda h, s: (h, batch_of(s), 0, 0)),
            pl.BlockSpec((2 * gd, None, TILE, DIFF_DK), lambda h, s: (h, nxt, 0, 0)),
            pl.BlockSpec((gd, None, dv_rows, TILE), lambda h, s: (h, nxt, 0, 0)),
            pl.BlockSpec((patterns.shape[0], 2 * gd, TILE, TILE), lambda h, s: (0, h, 0, 0)),
            pl.BlockSpec((1, DIFF_DK), lambda h, s: (0, 0)),
            pl.BlockSpec((1, DIFF_DK), lambda h, s: (0, 0)),
            pl.BlockSpec((1, DIFF_DK), lambda h, s: (0, 0)),
            pl.BlockSpec((1, DIFF_DK), lambda h, s: (0, 0)),
            pl.BlockSpec((1, DIFF_V), lambda h, s: (0, 0)),
        ],
        out_specs=pl.BlockSpec((TILE, gd * DIFF_V), lambda h, s: (s, h)),
        out_shape=jax.ShapeDtypeStruct((R, DIFF_HEADS * DIFF_V), BF16),
        scratch_shapes=[
            pltpu.VMEM((2 * gd, 1, TILE), F32),
            pltpu.VMEM((2 * gd, dv_rows, TILE), F32),
            pltpu.VMEM((2 * gd, TILE, TILE), F32),
            pltpu.VMEM((2 * gd, 1, TILE), F32),
            pltpu.VMEM((2 * gd, TILE, TILE), F32),
            pltpu.VMEM((2 * gd, 1, TILE), F32),
        ],
        compiler_params=pltpu.CompilerParams(dimension_semantics=("arbitrary", "arbitrary"),
                                             vmem_limit_bytes=VMEM_LIMIT),
        name="diff_attn",
    )(qd, kd, vdt, kd, vdt, bias_tiles, lambda_q1.astype(F32), lambda_k1.astype(F32),
      lambda_q2.astype(F32), lambda_k2.astype(F32), row1(g_diff_sub[0]))

    h1, n2 = pl.pallas_call(
        functools.partial(_oproj_body, nxt),
        grid=(nt,),
        in_specs=[
            pl.BlockSpec((TILE, MLA_HEADS * MLA_V), lambda m: (m, 0)),
            pl.BlockSpec((TILE, DIFF_HEADS * DIFF_V), lambda m: (m, 0)),
            pl.BlockSpec((d_mix, D), const2, **resident),
            pl.BlockSpec((TILE, D), x_idx),
            pl.BlockSpec((TILE, D), const2),
            pl.BlockSpec((1, D), const2),
            pl.BlockSpec((1, D), const2),
        ],
        out_specs=[
            pl.BlockSpec((TILE, D), lambda m: (m, 0)),
            pl.BlockSpec((TILE, D), lambda m: (m, 0)),
        ],
        out_shape=[
            jax.ShapeDtypeStruct((R, D), F32),
            jax.ShapeDtypeStruct((R, D), BF16),
        ],
        compiler_params=pltpu.CompilerParams(dimension_semantics=("arbitrary",), vmem_limit_bytes=VMEM_LIMIT),
        name="out_proj",
    )(o_m, o_d, w_o_b, x2, meta_pad, row1(g_attn_post[0]), row1(g_ffn_pre[0]))

    n_m = (B * S) // FFN_TM
    n_f = d_ff // FFN_TF
    tiles_per_batch_ffn = S // FFN_TM
    halo_per_tile = FFN_TM // FFN_HALO
    meta_halo_blk = (B * S) // FFN_HALO

    def halo_idx(m, f):
        return (jnp.where(lax.rem(m, tiles_per_batch_ffn) == 0, meta_halo_blk, m * halo_per_tile - 1), 0)

    out = pl.pallas_call(
        functools.partial(_ffn_body, n_f),
        grid=(n_m, n_f),
        in_specs=[
            pl.BlockSpec((FFN_TM, D), lambda m, f: (m, 0)),
            pl.BlockSpec((FFN_HALO, D), halo_idx),
            pl.BlockSpec((FFN_TM, D), lambda m, f: (m, 0)),
            pl.BlockSpec((D, FFN_TF), lambda m, f: (0, f)),
            pl.BlockSpec((D, FFN_TF), lambda m, f: (0, n_f + f)),
            pl.BlockSpec((FFN_TF, D), lambda m, f: (f, 0)),
            pl.BlockSpec((CONV_W, FFN_TF), lambda m, f: (0, f)),
            pl.BlockSpec((CONV_W, FFN_TF), lambda m, f: (0, n_f + f)),
            pl.BlockSpec((1, FFN_TF), lambda m, f: (0, f)),
            pl.BlockSpec((1, FFN_TF), lambda m, f: (0, n_f + f)),
            pl.BlockSpec((1, D), lambda m, f: (0, 0)),
        ],
        out_specs=pl.BlockSpec((FFN_TM, D), lambda m, f: (m, 0)),
        out_shape=jax.ShapeDtypeStruct((B * S, D), x.dtype),
        scratch_shapes=[
            pltpu.VMEM((FFN_HALO + FFN_TM, D), BF16),
            pltpu.VMEM((FFN_HALO + FFN_TM, FFN_TF), F32),
        ],
        compiler_params=pltpu.CompilerParams(dimension_semantics=("arbitrary", "arbitrary"),
                                             vmem_limit_bytes=VMEM_LIMIT),
        name="conv_ffn",
    )(n2, n2, h1, w_up_b, w_up_b, w_down_b, conv_w[0], conv_w[0], row1(conv_b[0]), row1(conv_b[0]),
      row1(g_ffn_post[0]))

    return out.reshape(B, S, D)
```

```python
import functools
import math

import numpy as np
import jax
import jax.numpy as jnp
from jax import lax
from jax.experimental import pallas as pl
from jax.experimental.pallas import tpu as pltpu

F32 = jnp.float32
BF16 = jnp.bfloat16

N_META = 16
MLA_HEADS = 8
MLA_NOPE = 128
MLA_ROPE = 64
MLA_V = 128
MLA_QK = MLA_NOPE + MLA_ROPE
Q_LORA = 512
KV_LORA = 256
ROPE_THETA = 10000.0
DIFF_HEADS = 4
DIFF_DK = 128
DIFF_V = 2 * DIFF_DK
DIFF_QK = DIFF_HEADS * 2 * DIFF_DK
NUM_BUCKETS = 32
REL_MAX_DIST = 128
CONV_W = 3
EPS = 1e-6
LOG2E = math.log2(math.e)

TILE = 256
QTILE = 2 * TILE
META_KEYS = 128
ONES_ROWS = 16
MLA_GROUP = 4
DIFF_GROUP = 2
FFN_TM = 512
FFN_TF = 512
FFN_HALO = 16
VMEM_LIMIT = 56 * 1024 * 1024


def _rms(x, g):
    ms = jnp.mean(x * x, axis=-1, keepdims=True)
    return x * lax.rsqrt(ms + EPS) * g


def _dot(a, b):
    return jnp.dot(a, b, preferred_element_type=F32)


def _dot_nt(a, b):
    return lax.dot_general(a, b, (((1,), (1,)), ((), ())), preferred_element_type=F32)


def _proj_body(n_x_tiles, q_scale_mla, q_scale_diff,
               x_ref, meta_ref, cos_ref, sin_ref, cs_ref, g_pre_ref, w_in_ref, g_cq_ref, g_ckv_ref,
               w_uq_ref, w_ukv_ref,
               qm_ref, km_ref, vmt_ref, qd_ref, kd_ref, vdt_ref,
               n_scr, proj_scr, qm_scr, kv_scr):
    m = pl.program_id(0)

    @pl.when(m < n_x_tiles)
    def _():
        n_scr[...] = _rms(x_ref[...], g_pre_ref[...]).astype(BF16)

    @pl.when(m >= n_x_tiles)
    def _():
        n_scr[...] = _rms(meta_ref[...], g_pre_ref[...]).astype(BF16)

    proj_scr[...] = _dot(n_scr[...], w_in_ref[...])
    c_q = _rms(proj_scr[:, 0:Q_LORA], g_cq_ref[...]).astype(BF16)
    c_kv = _rms(proj_scr[:, Q_LORA:Q_LORA + KV_LORA], g_ckv_ref[...]).astype(BF16)
    qm_scr[...] = _dot(c_q, w_uq_ref[...])
    kv_scr[...] = _dot(c_kv, w_ukv_ref[...])

    n_nope = MLA_HEADS * MLA_NOPE
    n_rope = MLA_HEADS * MLA_ROPE
    q_rope = (qm_scr[:, n_nope:n_nope + n_rope] * cos_ref[...]
              + qm_scr[:, n_nope + n_rope:n_nope + 2 * n_rope] * sin_ref[...])
    off_kr = Q_LORA + KV_LORA + 2 * DIFF_QK + DIFF_HEADS * DIFF_V
    kt = proj_scr[:, off_kr:off_kr + 2 * MLA_ROPE] * cs_ref[...]
    k_rope = (kt[:, 0:MLA_ROPE] + kt[:, MLA_ROPE:2 * MLA_ROPE]).astype(BF16)

    for h in range(MLA_HEADS):
        qm_ref[h, :, 0:MLA_NOPE] = (qm_scr[:, h * MLA_NOPE:(h + 1) * MLA_NOPE] * q_scale_mla).astype(BF16)
        qm_ref[h, :, MLA_NOPE:MLA_QK] = (q_rope[:, h * MLA_ROPE:(h + 1) * MLA_ROPE] * q_scale_mla).astype(BF16)
        km_ref[h, 0, :, 0:MLA_NOPE] = kv_scr[:, h * MLA_NOPE:(h + 1) * MLA_NOPE].astype(BF16)
        km_ref[h, 0, :, MLA_NOPE:MLA_QK] = k_rope
        v = kv_scr[:, n_nope + h * MLA_V:n_nope + (h + 1) * MLA_V]
        vmt_ref[h, 0, 0:MLA_V, :] = v.T.astype(BF16)
        vmt_ref[h, 0, MLA_V:MLA_V + ONES_ROWS, :] = jnp.ones((ONES_ROWS, TILE), BF16)

    off_qd = Q_LORA + KV_LORA
    off_kd = off_qd + DIFF_QK
    off_vd = off_kd + DIFF_QK
    for i in range(2 * DIFF_HEADS):
        qd_ref[i] = (proj_scr[:, off_qd + i * DIFF_DK:off_qd + (i + 1) * DIFF_DK] * q_scale_diff).astype(BF16)
        kd_ref[i, 0] = proj_scr[:, off_kd + i * DIFF_DK:off_kd + (i + 1) * DIFF_DK].astype(BF16)
    for h in range(DIFF_HEADS):
        v = proj_scr[:, off_vd + h * DIFF_V:off_vd + (h + 1) * DIFF_V]
        vdt_ref[h, 0, 0:DIFF_V, :] = v.T.astype(BF16)
        vdt_ref[h, 0, DIFF_V:DIFF_V + ONES_ROWS, :] = jnp.ones((ONES_ROWS, TILE), BF16)


def _meta_valid(is_x):
    row = lax.broadcasted_iota(jnp.int32, (META_KEYS, QTILE), 0)
    col = lax.broadcasted_iota(jnp.int32, (META_KEYS, QTILE), 1)
    return row <= jnp.where(is_x, N_META - 1, jnp.minimum(col, N_META - 1))


def _diag_valid(kind):
    if kind not in ("diag_lo", "diag_hi"):
        return None
    row = lax.broadcasted_iota(jnp.int32, (TILE, QTILE), 0)
    col = lax.broadcasted_iota(jnp.int32, (TILE, QTILE), 1)
    return row - col <= (0 if kind == "diag_lo" else -TILE)


def _chunk_pipeline(is_x, u, qk_meta, qk_x, update_meta, update_x, b0, b1):
    qk_meta(b0)

    @pl.when(jnp.logical_not(is_x))
    def _():
        update_meta(b0)

    @pl.when(is_x)
    def _():
        qk_x(b1, 2 * u + 1, "diag_hi")
        update_meta(b0)
        qk_x(b0, 2 * u, "diag_lo")
        update_x(b1, 2 * u + 1)

        @pl.when(u == 0)
        def _():
            update_x(b0, 2 * u)

        @pl.when(u >= 1)
        def _():
            qk_x(b1, 2 * u - 1, "prev")
            update_x(b0, 2 * u)

            def pair(k, carry):
                qk_x(b0, 2 * k, "far")
                update_x(b1, jnp.where(k == 0, 2 * u - 1, 2 * k - 1))
                qk_x(b1, 2 * k + 1, "far")
                update_x(b0, 2 * k)
                return carry

            lax.fori_loop(0, u - 1, pair, 0)
            qk_x(b0, 2 * u - 2, "far")
            update_x(b1, jnp.where(u == 1, 2 * u - 1, 2 * u - 3))
            update_x(b0, 2 * u - 2)


def _mla_body(n_x_tiles, tiles_per_batch,
              q_ref, k_ref, vt_ref, kmeta_ref, vtmeta_ref, o_ref,
              m_scr, acc_scr, st0_scr, mx0_scr, st1_scr, mx1_scr):
    n_heads = q_ref.shape[0]
    s = pl.program_id(1)
    is_x = s < n_x_tiles
    u = lax.rem(s, tiles_per_batch)

    for g in range(n_heads):
        m_scr[g] = jnp.full((1, QTILE), -jnp.inf, F32)
        acc_scr[g] = jnp.zeros((MLA_V + ONES_ROWS, QTILE), F32)

    def qk_meta(buf):
        st_buf, mx_buf = buf
        valid = _meta_valid(is_x)
        for g in range(n_heads):
            st = jnp.where(valid, _dot_nt(kmeta_ref[g, 0:META_KEYS, :], q_ref[g]), -jnp.inf)
            st_buf[g, 0:META_KEYS, :] = st
            mx_buf[g] = jnp.max(st, axis=0, keepdims=True)

    def qk_x(buf, j, kind):
        st_buf, mx_buf = buf
        valid = _diag_valid(kind)
        for g in range(n_heads):
            st = _dot_nt(k_ref[g, j], q_ref[g])
            if valid is not None:
                st = jnp.where(valid, st, -jnp.inf)
            st_buf[g] = st
            mx_buf[g] = jnp.max(st, axis=0, keepdims=True)

    def update(buf, rows, vt_of):
        st_buf, mx_buf = buf
        for g in range(n_heads):
            m_prev = m_scr[g]
            m_new = jnp.maximum(m_prev, mx_buf[g])
            alpha = jnp.exp2(m_prev - m_new)
            p = jnp.exp2(st_buf[g, 0:rows, :] - m_new)
            acc_scr[g] = alpha * acc_scr[g] + _dot(vt_of(g), p.astype(BF16))
            m_scr[g] = m_new

    def update_meta(buf):
        update(buf, META_KEYS, lambda g: vtmeta_ref[g, :, 0:META_KEYS])

    def update_x(buf, j):
        update(buf, TILE, lambda g: vt_ref[g, j])

    _chunk_pipeline(is_x, u, qk_meta, qk_x, update_meta, update_x, (st0_scr, mx0_scr), (st1_scr, mx1_scr))

    for g in range(n_heads):
        o_t = acc_scr[g, 0:MLA_V, :] / acc_scr[g, MLA_V:MLA_V + 1, :]
        o_ref[:, g * MLA_V:(g + 1) * MLA_V] = o_t.T.astype(BF16)


def _bias_body(idx_ref, rel_ref, out_ref):
    idx = idx_ref[0]
    n_maps = out_ref.shape[1]
    for mp in range(n_maps):
        far = jnp.full(idx.shape, rel_ref[NUM_BUCKETS - 1, mp], F32)
        acc = far
        for b in range(NUM_BUCKETS - 1):
            acc = jnp.where(idx == b, rel_ref[b, mp], acc)
        out_ref[0, mp] = (acc - far) * LOG2E


def _diff_body(n_x_tiles, tiles_per_batch, lam_init,
               q_ref, k_ref, vt_ref, kmeta_ref, vtmeta_ref, bias_ref,
               lq1_ref, lk1_ref, lq2_ref, lk2_ref, gsub_ref, o_ref,
               m_scr, acc_scr, st0_scr, mx0_scr, st1_scr, mx1_scr):
    n_maps = q_ref.shape[0]
    s = pl.program_id(1)
    is_x = s < n_x_tiles
    u = lax.rem(s, tiles_per_batch)

    for i in range(n_maps):
        m_scr[i] = jnp.full((1, QTILE), -jnp.inf, F32)
        acc_scr[i] = jnp.zeros((DIFF_V + ONES_ROWS, QTILE), F32)

    def qk_meta(buf):
        st_buf, mx_buf = buf
        valid = _meta_valid(is_x)
        w_first = jnp.where(jnp.logical_and(is_x, u == 0), 1.0, 0.0).astype(F32)
        w_meta = jnp.where(is_x, 0.0, 1.0).astype(F32)
        for i in range(n_maps):
            b = w_first * bias_ref[2, i, 0:META_KEYS, :] + w_meta * bias_ref[0, i, 0:META_KEYS, :]
            st = _dot_nt(kmeta_ref[i, 0:META_KEYS, :], q_ref[i])
            st = jnp.concatenate([st[:, 0:TILE] + b, st[:, TILE:QTILE]], axis=1)
            st = jnp.where(valid, st, -jnp.inf)
            st_buf[i, 0:META_KEYS, :] = st
            mx_buf[i] = jnp.max(st, axis=0, keepdims=True)

    def qk_x(buf, j, kind):
        st_buf, mx_buf = buf
        valid = _diag_valid(kind)
        for i in range(n_maps):
            st = _dot_nt(k_ref[i, j], q_ref[i])
            if kind == "diag_lo":
                st = st + jnp.concatenate([bias_ref[0, i], bias_ref[1, i]], axis=1)
            elif kind == "diag_hi":
                st = st + jnp.concatenate([bias_ref[0, i], bias_ref[0, i]], axis=1)
            elif kind == "prev":
                st = jnp.concatenate([st[:, 0:TILE] + bias_ref[1, i], st[:, TILE:QTILE]], axis=1)
            if valid is not None:
                st = jnp.where(valid, st, -jnp.inf)
            st_buf[i] = st
            mx_buf[i] = jnp.max(st, axis=0, keepdims=True)

    def update(buf, rows, vt_of):
        st_buf, mx_buf = buf
        for i in range(n_maps):
            m_prev = m_scr[i]
            m_new = jnp.maximum(m_prev, mx_buf[i])
            alpha = jnp.exp2(m_prev - m_new)
            p = jnp.exp2(st_buf[i, 0:rows, :] - m_new)
            acc_scr[i] = alpha * acc_scr[i] + _dot(vt_of(i // 2), p.astype(BF16))
            m_scr[i] = m_new

    def update_meta(buf):
        update(buf, META_KEYS, lambda h: vtmeta_ref[h, :, 0:META_KEYS])

    def update_x(buf, j):
        update(buf, TILE, lambda h: vt_ref[h, j])

    _chunk_pipeline(is_x, u, qk_meta, qk_x, update_meta, update_x, (st0_scr, mx0_scr), (st1_scr, mx1_scr))

    lam =(jnp.exp(jnp.sum(lq1_ref[...] * lk1_ref[...], axis=-1, keepdims=True))
           - jnp.exp(jnp.sum(lq2_ref[...] * lk2_ref[...], axis=-1, keepdims=True)) + lam_init)
    for h in range(n_maps // 2):
        a0 = acc_scr[2 * h, 0:DIFF_V, :] / acc_scr[2 * h, DIFF_V:DIFF_V + 1, :]
        a1 = acc_scr[2 * h + 1, 0:DIFF_V, :] / acc_scr[2 * h + 1, DIFF_V:DIFF_V + 1, :]
        o_t = a0 - lam * a1
        o = o_t.T
        o_ref[:, h * DIFF_V:(h + 1) * DIFF_V] = (_rms(o, gsub_ref[...]) * (1.0 - lam_init)).astype(BF16)


def _oproj_body(n_x_tiles, om_ref, od_ref, wo_ref, x_ref, meta_ref, g_post_ref, g_pre_ref, h1_ref, n2_ref):
    m = pl.program_id(0)
    n_m = om_ref.shape[1]
    a = _dot(om_ref[...], wo_ref[0:n_m, :]) + _dot(od_ref[...], wo_ref[n_m:, :])
    h1_ref[...] = _rms(a, g_post_ref[...])

    @pl.when(m < n_x_tiles)
    def _():
        h1_ref[...] += x_ref[...]

    @pl.when(m == n_x_tiles)
    def _():
        h1_ref[...] += meta_ref[...]

    n2_ref[...] = _rms(h1_ref[...], g_pre_ref[...]).astype(BF16)


def _ffn_body(n_f,
              n2_ref, halo_ref, h1_ref, wg_ref, wv_ref, wd_ref, cwg_ref, cwv_ref, cbg_ref, cbv_ref, g_ref,
              out_ref, n2e_scr, upg_scr, upv_scr):
    f = pl.program_id(1)
    tm = n2_ref.shape[0]

    @pl.when(f == 0)
    def _():
        n2e_scr[0:FFN_HALO, :] = halo_ref[...]
        n2e_scr[FFN_HALO:, :] = n2_ref[...]
        out_ref[...] = jnp.zeros(out_ref.shape, F32)

    upg_scr[...] = _dot(n2e_scr[...], wg_ref[...])
    upv_scr[...] = _dot(n2e_scr[...], wv_ref[...])

    def conv(up_scr, cw_ref, cb_ref):
        acc = cb_ref[...] + cw_ref[0:1, :] * up_scr[FFN_HALO - 2:FFN_HALO - 2 + tm, :]
        acc = acc + cw_ref[1:2, :] * up_scr[FFN_HALO - 1:FFN_HALO - 1 + tm, :]
        return acc + cw_ref[2:3, :] * up_scr[FFN_HALO:FFN_HALO + tm, :]

    gate = conv(upg_scr, cwg_ref, cbg_ref)
    val = conv(upv_scr, cwv_ref, cbv_ref)
    act = (gate / (1.0 + jnp.exp(-gate)) * val).astype(BF16)
    out_ref[...] += _dot(act, wd_ref[...])

    @pl.when(f == n_f - 1)
    def _():
        out_ref[...] = h1_ref[...] + _rms(out_ref[...], g_ref[...])


def _t5_bucket_np(n):
    n = np.maximum(n, 0)
    max_exact = NUM_BUCKETS // 2
    nf = np.maximum(n, 1).astype(np.float32)
    large = max_exact + (np.log(nf / np.float32(max_exact)) / np.float32(math.log(REL_MAX_DIST / max_exact))
                         * np.float32(NUM_BUCKETS - max_exact)).astype(np.int32)
    large = np.minimum(large, NUM_BUCKETS - 1)
    return np.where(n < max_exact, n, large).astype(np.int32)


def _bias_patterns():
    kk = np.arange(TILE)[:, None]
    qq = np.arange(TILE)[None, :]
    return np.stack([_t5_bucket_np(qq - kk), _t5_bucket_np(TILE + qq - kk), _t5_bucket_np(qq + N_META - kk)])


def kernel(x, meta_tokens, rel_bias, w_in, w_uq, w_ukv, w_o, g_attn_pre, g_attn_post, g_cq, g_ckv,
           lambda_q1, lambda_k1, lambda_q2, lambda_k2, g_diff_sub, g_ffn_pre, g_ffn_post,
           w_up, conv_w, conv_b, w_down):
    B, S, D = x.shape
    assert w_in.shape[0] == 1, "single-layer problem"
    assert S % QTILE == 0 and S % FFN_TM == 0
    d_ff = w_down.shape[1]
    assert d_ff % FFN_TF == 0
    tpb = S // TILE
    nxt = B * tpb
    n_meta_tiles = QTILE // TILE
    nt = nxt + n_meta_tiles
    R = nt * TILE
    qpb = S // QTILE
    nxq = B * qpb
    n_bias_maps = rel_bias.shape[1]
    lam_init = 0.8 - 0.6 * math.exp(-0.3 * 0)
    d_mix = MLA_HEADS * MLA_V + DIFF_HEADS * DIFF_V
    mv_rows = MLA_V + ONES_ROWS
    dv_rows = DIFF_V + ONES_ROWS

    def rot_cols(w):
        half = w.shape[-1] // 2
        return jnp.concatenate([-w[..., half:], w[..., :half]], axis=-1)

    wi = w_in[0]
    c0 = Q_LORA + KV_LORA
    k_r = wi[:, c0:c0 + MLA_ROPE]
    w_in_p = jnp.concatenate([wi[:, :c0], wi[:, c0 + MLA_ROPE:], k_r, rot_cols(k_r)], axis=1).astype(BF16)
    wq = w_uq[0].reshape(Q_LORA, MLA_HEADS, MLA_QK)
    q_rope_w = wq[:, :, MLA_NOPE:]
    w_uq_p = jnp.concatenate([wq[:, :, :MLA_NOPE].reshape(Q_LORA, -1), q_rope_w.reshape(Q_LORA, -1),
                              rot_cols(q_rope_w).reshape(Q_LORA, -1)], axis=1).astype(BF16)
    wkv = w_ukv[0].reshape(KV_LORA, MLA_HEADS, MLA_NOPE + MLA_V)
    w_ukv_p = jnp.concatenate([wkv[:, :, :MLA_NOPE].reshape(KV_LORA, -1),
                               wkv[:, :, MLA_NOPE:].reshape(KV_LORA, -1)], axis=1).astype(BF16)
    w_o_b = w_o[0].astype(BF16)
    w_up_b = w_up[0].astype(BF16)
    w_down_b = w_down[0].astype(BF16)

    half = MLA_ROPE // 2
    inv = ROPE_THETA ** (-jnp.arange(half, dtype=F32) / half)
    pos = jnp.concatenate([jnp.arange(S, dtype=jnp.int32) + N_META, jnp.arange(QTILE, dtype=jnp.int32)])
    ang = pos.astype(F32)[:, None] * inv[None, :]
    cos64 = jnp.tile(jnp.cos(ang), (1, 2))
    sin64 = jnp.tile(jnp.sin(ang), (1, 2))
    cos_q = jnp.tile(cos64, (1, MLA_HEADS))
    sin_q = jnp.tile(sin64, (1, MLA_HEADS))
    cs_k = jnp.concatenate([cos64, sin64], axis=1)

    x2 = x.reshape(B * S, D)
    meta_pad = jnp.concatenate([meta_tokens.astype(x.dtype), jnp.zeros((QTILE - N_META, D), x.dtype)], axis=0)
    row1 = lambda v: v.reshape(1, -1)

    x_idx = lambda m: (jnp.minimum(m, nxt - 1), 0)
    meta_idx = lambda m: (jnp.maximum(m - nxt, 0), 0)
    tab_idx = lambda m: (jnp.where(m < nxt, lax.rem(m, tpb), tpb + m - nxt), 0)
    const2 = lambda m: (0, 0)
    d_in_p = w_in_p.shape[1]
    resident = dict(pipeline_mode=pl.Buffered(1))

    qm, km, vmt, qd, kd, vdt = pl.pallas_call(
        functools.partial(_proj_body, nxt, LOG2E / math.sqrt(MLA_QK), LOG2E / math.sqrt(DIFF_DK)),
        grid=(nt,),
        in_specs=[
            pl.BlockSpec((TILE, D), x_idx),
            pl.BlockSpec((TILE, D), meta_idx),
            pl.BlockSpec((TILE, MLA_HEADS * MLA_ROPE), tab_idx),
            pl.BlockSpec((TILE, MLA_HEADS * MLA_ROPE), tab_idx),
            pl.BlockSpec((TILE, 2 * MLA_ROPE), tab_idx),
            pl.BlockSpec((1, D), const2),
            pl.BlockSpec((D, d_in_p), const2, **resident),
            pl.BlockSpec((1, Q_LORA), const2),
            pl.BlockSpec((1, KV_LORA), const2),
            pl.BlockSpec(w_uq_p.shape, const2, **resident),
            pl.BlockSpec(w_ukv_p.shape, const2, **resident),
        ],
        out_specs=[
            pl.BlockSpec((MLA_HEADS, TILE, MLA_QK), lambda m: (0, m, 0)),
            pl.BlockSpec((MLA_HEADS, 1, TILE, MLA_QK), lambda m: (0, m, 0, 0)),
            pl.BlockSpec((MLA_HEADS, 1, mv_rows, TILE), lambda m: (0, m, 0, 0)),
            pl.BlockSpec((2 * DIFF_HEADS, TILE, DIFF_DK), lambda m: (0, m, 0)),
            pl.BlockSpec((2 * DIFF_HEADS, 1, TILE, DIFF_DK), lambda m: (0, m, 0, 0)),
            pl.BlockSpec((DIFF_HEADS, 1, dv_rows, TILE), lambda m: (0, m, 0, 0)),
        ],
        out_shape=[
            jax.ShapeDtypeStruct((MLA_HEADS, R, MLA_QK), BF16),
            jax.ShapeDtypeStruct((MLA_HEADS, nt, TILE, MLA_QK), BF16),
            jax.ShapeDtypeStruct((MLA_HEADS, nt, mv_rows, TILE), BF16),
            jax.ShapeDtypeStruct((2 * DIFF_HEADS, R, DIFF_DK), BF16),
            jax.ShapeDtypeStruct((2 * DIFF_HEADS, nt, TILE, DIFF_DK), BF16),
            jax.ShapeDtypeStruct((DIFF_HEADS, nt, dv_rows, TILE), BF16),
        ],
        scratch_shapes=[
            pltpu.VMEM((TILE, D), BF16),
            pltpu.VMEM((TILE, d_in_p), F32),
            pltpu.VMEM((TILE, w_uq_p.shape[1]), F32),
            pltpu.VMEM((TILE, w_ukv_p.shape[1]), F32),
        ],
        compiler_params=pltpu.CompilerParams(dimension_semantics=("arbitrary",), vmem_limit_bytes=VMEM_LIMIT),
        name="in_proj",
    )(x2, meta_pad, cos_q, sin_q, cs_k, row1(g_attn_pre[0]), w_in_p, row1(g_cq[0]), row1(g_ckv[0]),
      w_uq_p, w_ukv_p)

    batch_of = lambda s: jnp.minimum(s // qpb, B - 1)

    gm = MLA_GROUP
    o_m = pl.pallas_call(
        functools.partial(_mla_body, nxq, qpb),
        grid=(MLA_HEADS // gm, nxq + 1),
        in_specs=[
            pl.BlockSpec((gm, QTILE, MLA_QK), lambda h, s: (h, s, 0)),
            pl.BlockSpec((gm, tpb, TILE, MLA_QK), lambda h, s: (h, batch_of(s), 0, 0)),
            pl.BlockSpec((gm, tpb, mv_rows, TILE), lambda h, s: (h, batch_of(s), 0, 0)),
            pl.BlockSpec((gm, None, TILE, MLA_QK), lambda h, s: (h, nxt, 0, 0)),
            pl.BlockSpec((gm, None, mv_rows, TILE), lambda h, s: (h, nxt, 0, 0)),
        ],
        out_specs=pl.BlockSpec((QTILE, gm * MLA_V), lambda h, s: (s, h)),
        out_shape=jax.ShapeDtypeStruct((R, MLA_HEADS * MLA_V), BF16),
        scratch_shapes=[
            pltpu.VMEM((gm, 1, QTILE), F32),
            pltpu.VMEM((gm, mv_rows, QTILE), F32),
            pltpu.VMEM((gm, TILE, QTILE), F32),
            pltpu.VMEM((gm, 1, QTILE), F32),
            pltpu.VMEM((gm, TILE, QTILE), F32),
            pltpu.VMEM((gm, 1, QTILE), F32),
        ],
        compiler_params=pltpu.CompilerParams(dimension_semantics=("arbitrary", "arbitrary"),
                                             vmem_limit_bytes=VMEM_LIMIT),
        name="mla_attn",
    )(qm, km, vmt, km, vmt)

    patterns = jnp.asarray(_bias_patterns())
    bias_tiles = pl.pallas_call(
        _bias_body,
        grid=(patterns.shape[0],),
        in_specs=[
            pl.BlockSpec((1, TILE, TILE), lambda p: (p, 0, 0)),
            pl.BlockSpec(memory_space=pltpu.SMEM),
        ],
        out_specs=pl.BlockSpec((1, n_bias_maps, TILE, TILE), lambda p: (p, 0, 0, 0)),
        out_shape=jax.ShapeDtypeStruct((patterns.shape[0], n_bias_maps, TILE, TILE), F32),
        compiler_params=pltpu.CompilerParams(dimension_semantics=("arbitrary",)),
        name="rel_bias_tiles",
    )(patterns, rel_bias.astype(F32))

    gd = DIFF_GROUP
    o_d = pl.pallas_call(
        functools.partial(_diff_body, nxq, qpb, lam_init),
        grid=(DIFF_HEADS // gd, nxq + 1),
        in_specs=[
            pl.BlockSpec((2 * gd, QTILE, DIFF_DK), lambda h, s: (h, s, 0)),
            pl.BlockSpec((2 * gd, tpb, TILE, DIFF_DK), lambda h, s: (h, batch_of(s), 0, 0)),
            pl.BlockSpec((gd, tpb, dv_rows, TILE), lambda h, s: (h, batch_of(s), 0, 0)),
            pl.BlockSpec((2 * gd, None, TILE, DIFF_DK), lambda h, s: (h, nxt, 0, 0)),
            pl.BlockSpec((gd, None, dv_rows, TILE), lambda h, s: (h, nxt, 0, 0)),
            pl.BlockSpec((patterns.shape[0], 2 * gd, TILE, TILE), lambda h, s: (0, h, 0, 0)),
            pl.BlockSpec((1, DIFF_DK), lambda h, s: (0, 0)),
            pl.BlockSpec((1, DIFF_DK), lambda h, s: (0, 0)),
            pl.BlockSpec((1, DIFF_DK), lambda h, s: (0, 0)),
            pl.BlockSpec((1, DIFF_DK), lambda h, s: (0, 0)),
            pl.BlockSpec((1, DIFF_V), lambda h, s: (0, 0)),
        ],
        out_specs=pl.BlockSpec((QTILE, gd * DIFF_V), lambda h, s: (s, h)),
        out_shape=jax.ShapeDtypeStruct((R, DIFF_HEADS * DIFF_V), BF16),
        scratch_shapes=[
            pltpu.VMEM((2 * gd, 1, QTILE), F32),
            pltpu.VMEM((2 * gd, dv_rows, QTILE), F32),
            pltpu.VMEM((2 * gd, TILE, QTILE), F32),
            pltpu.VMEM((2 * gd, 1, QTILE), F32),
            pltpu.VMEM((2 * gd, TILE, QTILE), F32),
            pltpu.VMEM((2 * gd, 1, QTILE), F32),
        ],
        compiler_params=pltpu.CompilerParams(dimension_semantics=("arbitrary", "arbitrary"),
                                             vmem_limit_bytes=VMEM_LIMIT),
        name="diff_attn",
    )(qd, kd, vdt, kd, vdt, bias_tiles, lambda_q1.astype(F32), lambda_k1.astype(F32),
      lambda_q2.astype(F32), lambda_k2.astype(F32), row1(g_diff_sub[0]))

    nt_o = nxt + 1
    h1, n2 = pl.pallas_call(
        functools.partial(_oproj_body, nxt),
        grid=(nt_o,),
        in_specs=[
            pl.BlockSpec((TILE, MLA_HEADS * MLA_V), lambda m: (m, 0)),
            pl.BlockSpec((TILE, DIFF_HEADS * DIFF_V), lambda m: (m, 0)),
            pl.BlockSpec((d_mix, D), const2, **resident),
            pl.BlockSpec((TILE, D), x_idx),
            pl.BlockSpec((TILE, D), const2),
            pl.BlockSpec((1, D), const2),
            pl.BlockSpec((1, D), const2),
        ],
        out_specs=[
            pl.BlockSpec((TILE, D), lambda m: (m, 0)),
            pl.BlockSpec((TILE, D), lambda m: (m, 0)),
        ],
        out_shape=[
            jax.ShapeDtypeStruct((nt_o * TILE, D), F32),
            jax.ShapeDtypeStruct((nt_o * TILE, D), BF16),
        ],
        compiler_params=pltpu.CompilerParams(dimension_semantics=("arbitrary",), vmem_limit_bytes=VMEM_LIMIT),
        name="out_proj",
    )(o_m, o_d, w_o_b, x2, meta_pad, row1(g_attn_post[0]), row1(g_ffn_pre[0]))

    n_m = (B * S) // FFN_TM
    n_f = d_ff // FFN_TF
    tiles_per_batch_ffn = S // FFN_TM
    halo_per_tile = FFN_TM // FFN_HALO
    meta_halo_blk = (B * S) // FFN_HALO

    def halo_idx(m, f):
        return (jnp.where(lax.rem(m, tiles_per_batch_ffn) == 0, meta_halo_blk, m * halo_per_tile - 1), 0)

    out = pl.pallas_call(
        functools.partial(_ffn_body, n_f),
        grid=(n_m, n_f),
        in_specs=[
            pl.BlockSpec((FFN_TM, D), lambda m, f: (m, 0)),
            pl.BlockSpec((FFN_HALO, D), halo_idx),
            pl.BlockSpec((FFN_TM, D), lambda m, f: (m, 0)),
            pl.BlockSpec((D, FFN_TF), lambda m, f: (0, f)),
            pl.BlockSpec((D, FFN_TF), lambda m, f: (0, n_f + f)),
            pl.BlockSpec((FFN_TF, D), lambda m, f: (f, 0)),
            pl.BlockSpec((CONV_W, FFN_TF), lambda m, f: (0, f)),
            pl.BlockSpec((CONV_W, FFN_TF), lambda m, f: (0, n_f + f)),
            pl.BlockSpec((1, FFN_TF), lambda m, f: (0, f)),
            pl.BlockSpec((1, FFN_TF), lambda m, f: (0, n_f + f)),
            pl.BlockSpec((1, D), lambda m, f: (0, 0)),
        ],
        out_specs=pl.BlockSpec((FFN_TM, D), lambda m, f: (m, 0)),
        out_shape=jax.ShapeDtypeStruct((B * S, D), x.dtype),
        scratch_shapes=[
            pltpu.VMEM((FFN_HALO + FFN_TM, D), BF16),
            pltpu.VMEM((FFN_HALO + FFN_TM, FFN_TF), F32),
            pltpu.VMEM((FFN_HALO + FFN_TM, FFN_TF), F32),
        ],
        compiler_params=pltpu.CompilerParams(dimension_semantics=("arbitrary", "arbitrary"),
                                             vmem_limit_bytes=VMEM_LIMIT),
        name="conv_ffn",
    )(n2, n2, h1, w_up_b, w_up_b, w_down_b, conv_w[0], conv_w[0], row1(conv_b[0]), row1(conv_b[0]),
      row1(g_ffn_post[0]))

    return out.reshape(B, S, D)
```

```python
import functools
import math

import numpy as np
import jax
import jax.numpy as jnp
from jax import lax
from jax.experimental import pallas as pl
from jax.experimental.pallas import tpu as pltpu

F32 = jnp.float32
BF16 = jnp.bfloat16

N_META = 16
MLA_HEADS = 8
MLA_NOPE = 128
MLA_ROPE = 64
MLA_V = 128
MLA_QK = MLA_NOPE + MLA_ROPE
Q_LORA = 512
KV_LORA = 256
ROPE_THETA = 10000.0
DIFF_HEADS = 4
DIFF_DK = 128
DIFF_V = 2 * DIFF_DK
DIFF_QK = DIFF_HEADS * 2 * DIFF_DK
NUM_BUCKETS = 32
REL_MAX_DIST = 128
CONV_W = 3
EPS = 1e-6
LOG2E = math.log2(math.e)

TILE = 256
QTILE = 2 * TILE
META_KEYS = 128
ONES_ROWS = 16
MLA_GROUP = 4
DIFF_GROUP = 2
FFN_TM = 512
FFN_TF = 512
FFN_HALO = 16
VMEM_LIMIT = 56 * 1024 * 1024


def _rms(x, g):
    ms = jnp.mean(x * x, axis=-1, keepdims=True)
    return x * lax.rsqrt(ms + EPS) * g


def _dot(a, b):
    return jnp.dot(a, b, preferred_element_type=F32)


def _dot_nt(a, b):
    return lax.dot_general(a, b, (((1,), (1,)), ((), ())), preferred_element_type=F32)


def _proj_body(n_x_tiles, q_scale_mla, q_scale_diff,
               x_ref, meta_ref, cos_ref, sin_ref, cs_ref, g_pre_ref, wa_ref, wb_ref, g_cq_ref, g_ckv_ref,
               w_uq_ref, w_ukv_ref, wcast_ref,
               qm_ref, km_ref, vmt_ref, qd_ref, kd_ref, vdt_ref, wcast_out_ref,
               n_scr, proj_scr, qm_scr, kv_scr):
    m = pl.program_id(0)
    wcast_out_ref[...] = wcast_ref[...].astype(BF16)

    @pl.when(m < n_x_tiles)
    def _():
        n_scr[...] = _rms(x_ref[...], g_pre_ref[...]).astype(BF16)

    @pl.when(m >= n_x_tiles)
    def _():
        n_scr[...] = _rms(meta_ref[...], g_pre_ref[...]).astype(BF16)

    n_lat = Q_LORA + KV_LORA
    proj_scr[:, 0:n_lat] = _dot(n_scr[...], wa_ref[...])
    proj_scr[:, n_lat:] = _dot(n_scr[...], wb_ref[...])
    c_q = _rms(proj_scr[:, 0:Q_LORA], g_cq_ref[...]).astype(BF16)
    c_kv = _rms(proj_scr[:, Q_LORA:n_lat], g_ckv_ref[...]).astype(BF16)
    qm_scr[...] = _dot(c_q, w_uq_ref[...])
    kv_scr[...] = _dot(c_kv, w_ukv_ref[...])

    n_nope = MLA_HEADS * MLA_NOPE
    n_rope = MLA_HEADS * MLA_ROPE
    lanes = cos_ref.shape[1]
    q_rope = jnp.concatenate(
        [qm_scr[:, n_nope + c:n_nope + c + lanes] * cos_ref[...]
         + qm_scr[:, n_nope + n_rope + c:n_nope + n_rope + c + lanes] * sin_ref[...]
         for c in range(0, n_rope, lanes)], axis=1)
    kt = proj_scr[:, n_lat:n_lat + 2 * MLA_ROPE] * cs_ref[...]
    k_rope = (kt[:, 0:MLA_ROPE] + kt[:, MLA_ROPE:2 * MLA_ROPE]).astype(BF16)

    for h in range(MLA_HEADS):
        qm_ref[h, :, 0:MLA_NOPE] = (qm_scr[:, h * MLA_NOPE:(h + 1) * MLA_NOPE] * q_scale_mla).astype(BF16)
        qm_ref[h, :, MLA_NOPE:MLA_QK] = (q_rope[:, h * MLA_ROPE:(h + 1) * MLA_ROPE] * q_scale_mla).astype(BF16)
        km_ref[h, 0, :, 0:MLA_NOPE] = kv_scr[:, h * MLA_NOPE:(h + 1) * MLA_NOPE].astype(BF16)
        km_ref[h, 0, :, MLA_NOPE:MLA_QK] = k_rope
        v = kv_scr[:, n_nope + h * MLA_V:n_nope + (h + 1) * MLA_V]
        vmt_ref[h, 0, 0:MLA_V, :] = v.T.astype(BF16)
        vmt_ref[h, 0, MLA_V:MLA_V + ONES_ROWS, :] = jnp.ones((ONES_ROWS, TILE), BF16)

    off_qd = n_lat + 2 * MLA_ROPE
    off_kd = off_qd + DIFF_QK
    off_vd = off_kd + DIFF_QK
    for i in range(2 * DIFF_HEADS):
        qd_ref[i] = (proj_scr[:, off_qd + i * DIFF_DK:off_qd + (i + 1) * DIFF_DK] * q_scale_diff).astype(BF16)
        kd_ref[i, 0] = proj_scr[:, off_kd + i * DIFF_DK:off_kd + (i + 1) * DIFF_DK].astype(BF16)
    for h in range(DIFF_HEADS):
        v = proj_scr[:, off_vd + h * DIFF_V:off_vd + (h + 1) * DIFF_V]
        vdt_ref[h, 0, 0:DIFF_V, :] = v.T.astype(BF16)
        vdt_ref[h, 0, DIFF_V:DIFF_V + ONES_ROWS, :] = jnp.ones((ONES_ROWS, TILE), BF16)


def _meta_valid(is_x):
    row = lax.broadcasted_iota(jnp.int32, (META_KEYS, QTILE), 0)
    col = lax.broadcasted_iota(jnp.int32, (META_KEYS, QTILE), 1)
    return row <= jnp.where(is_x, N_META - 1, jnp.minimum(col, N_META - 1))


def _diag_valid(kind):
    if kind not in ("diag_lo", "diag_hi"):
        return None
    row = lax.broadcasted_iota(jnp.int32, (TILE, QTILE), 0)
    col = lax.broadcasted_iota(jnp.int32, (TILE, QTILE), 1)
    return row - col <= (0 if kind == "diag_lo" else -TILE)


def _chunk_pipeline(is_x, u, qk_meta, qk_x, update_meta, update_x, b0, b1):
    qk_meta(b0)

    @pl.when(jnp.logical_not(is_x))
    def _():
        update_meta(b0)

    @pl.when(is_x)
    def _():
        qk_x(b1, 2 * u + 1, "diag_hi")
        update_meta(b0)
        qk_x(b0, 2 * u, "diag_lo")
        update_x(b1, 2 * u + 1)

        @pl.when(u == 0)
        def _():
            update_x(b0, 2 * u)

        @pl.when(u >= 1)
        def _():
            qk_x(b1, 2 * u - 1, "prev")
            update_x(b0, 2 * u)

            def pair(k, carry):
                qk_x(b0, 2 * k, "far")
                update_x(b1, jnp.where(k == 0, 2 * u - 1, 2 * k - 1))
                qk_x(b1, 2 * k + 1, "far")
                update_x(b0, 2 * k)
                return carry

            lax.fori_loop(0, u - 1, pair, 0)
            qk_x(b0, 2 * u - 2, "far")
            update_x(b1, jnp.where(u == 1, 2 * u - 1, 2 * u - 3))
            update_x(b0, 2 * u - 2)


def _mla_body(n_x_tiles, tiles_per_batch,
              q_ref, k_ref, vt_ref, kmeta_ref, vtmeta_ref, wcast_ref, o_ref, wcast_out_ref,
              m_scr, acc_scr, st0_scr, mx0_scr, st1_scr, mx1_scr):
    n_heads = q_ref.shape[0]
    s = pl.program_id(1)
    is_x = s < n_x_tiles
    u = lax.rem(s, tiles_per_batch)
    wcast_out_ref[...] = wcast_ref[...].astype(BF16)

    for g in range(n_heads):
        m_scr[g] = jnp.full((1, QTILE), -jnp.inf, F32)
        acc_scr[g] = jnp.zeros((MLA_V + ONES_ROWS, QTILE), F32)

    def qk_meta(buf):
        st_buf, mx_buf = buf
        valid = _meta_valid(is_x)
        for g in range(n_heads):
            st = jnp.where(valid, _dot_nt(kmeta_ref[g, 0:META_KEYS, :], q_ref[g]), -jnp.inf)
            st_buf[g, 0:META_KEYS, :] = st
            mx_buf[g] = jnp.max(st, axis=0, keepdims=True)

    def qk_x(buf, j, kind):
        st_buf, mx_buf = buf
        valid = _diag_valid(kind)
        for g in range(n_heads):
            st = _dot_nt(k_ref[g, j], q_ref[g])
            if valid is not None:
                st = jnp.where(valid, st, -jnp.inf)
            st_buf[g] = st
            mx_buf[g] = jnp.max(st, axis=0, keepdims=True)

    def update(buf, rows, vt_of):
        st_buf, mx_buf = buf
        for g in range(n_heads):
            m_prev = m_scr[g]
            m_new = jnp.maximum(m_prev, mx_buf[g])
            alpha = jnp.exp2(m_prev - m_new)
            p = jnp.exp2(st_buf[g, 0:rows, :] - m_new)
            acc_scr[g] = alpha * acc_scr[g] + _dot(vt_of(g), p.astype(BF16))
            m_scr[g] = m_new

    def update_meta(buf):
        update(buf, META_KEYS, lambda g: vtmeta_ref[g, :, 0:META_KEYS])

    def update_x(buf, j):
        update(buf, TILE, lambda g: vt_ref[g, j])

    _chunk_pipeline(is_x, u, qk_meta, qk_x, update_meta, update_x, (st0_scr, mx0_scr), (st1_scr, mx1_scr))

    for g in range(n_heads):
        o_t = acc_scr[g, 0:MLA_V, :] / acc_scr[g, MLA_V:MLA_V + 1, :]
        o_ref[:, g * MLA_V:(g + 1) * MLA_V] = o_t.T.astype(BF16)


def _bias_body(idx_ref, rel_ref, out_ref):
    idx = idx_ref[0]
    n_maps = out_ref.shape[1]
    for mp in range(n_maps):
        far = jnp.full(idx.shape, rel_ref[NUM_BUCKETS - 1, mp], F32)
        acc = far
        for b in range(NUM_BUCKETS - 1):
            acc = jnp.where(idx == b, rel_ref[b, mp], acc)
        out_ref[0, mp] = (acc - far) * LOG2E


def _diff_body(n_x_tiles, tiles_per_batch, lam_init,
               q_ref, k_ref, vt_ref, kmeta_ref, vtmeta_ref, bias_ref,
               lq1_ref, lk1_ref, lq2_ref, lk2_ref, gsub_ref, wcast_ref, o_ref, wcast_out_ref,
               m_scr, acc_scr, st0_scr, mx0_scr, st1_scr, mx1_scr):
    n_maps = q_ref.shape[0]
    s = pl.program_id(1)
    is_x = s < n_x_tiles
    u = lax.rem(s, tiles_per_batch)
    wcast_out_ref[...] = wcast_ref[...].astype(BF16)

    for i in range(n_maps):
        m_scr[i] = jnp.full((1, QTILE), -jnp.inf, F32)
        acc_scr[i] = jnp.zeros((DIFF_V + ONES_ROWS, QTILE), F32)

    def qk_meta(buf):
        st_buf, mx_buf = buf
        valid = _meta_valid(is_x)
        w_first = jnp.where(jnp.logical_and(is_x, u == 0), 1.0, 0.0).astype(F32)
        w_meta = jnp.where(is_x, 0.0, 1.0).astype(F32)
        for i in range(n_maps):
            b = w_first * bias_ref[2, i, 0:META_KEYS, :] + w_meta * bias_ref[0, i, 0:META_KEYS, :]
            st = _dot_nt(kmeta_ref[i, 0:META_KEYS, :], q_ref[i])
            st = jnp.concatenate([st[:, 0:TILE] + b, st[:, TILE:QTILE]], axis=1)
            st = jnp.where(valid, st, -jnp.inf)
            st_buf[i, 0:META_KEYS, :] = st
            mx_buf[i] = jnp.max(st, axis=0, keepdims=True)

    def qk_x(buf, j, kind):
        st_buf, mx_buf = buf
        valid = _diag_valid(kind)
        for i in range(n_maps):
            st = _dot_nt(k_ref[i, j], q_ref[i])
            if kind == "diag_lo":
                st = st + jnp.concatenate([bias_ref[0, i], bias_ref[1, i]], axis=1)
            elif kind == "diag_hi":
                st = st + jnp.concatenate([bias_ref[0, i], bias_ref[0, i]], axis=1)
            elif kind == "prev":
                st = jnp.concatenate([st[:, 0:TILE] + bias_ref[1, i], st[:, TILE:QTILE]], axis=1)
            if valid is not None:
                st = jnp.where(valid, st, -jnp.inf)
            st_buf[i] = st
            mx_buf[i] = jnp.max(st, axis=0, keepdims=True)

    def update(buf, rows, vt_of):
        st_buf, mx_buf = buf
        for i in range(n_maps):
            m_prev = m_scr[i]
            m_new = jnp.maximum(m_prev, mx_buf[i])
            alpha = jnp.exp2(m_prev - m_new)
            p = jnp.exp2(st_buf[i, 0:rows, :] - m_new)
            acc_scr[i] = alpha * acc_scr[i] + _dot(vt_of(i // 2), p.astype(BF16))
            m_scr[i] = m_new

    def update_meta(buf):
        update(buf, META_KEYS, lambda h: vtmeta_ref[h, :, 0:META_KEYS])

    def update_x(buf, j):
        update(buf, TILE, lambda h: vt_ref[h, j])

    _chunk_pipeline(is_x, u, qk_meta, qk_x, update_meta, update_x, (st0_scr, mx0_scr), (st1_scr, mx1_scr))

    lam =(jnp.exp(jnp.sum(lq1_ref[...] * lk1_ref[...], axis=-1, keepdims=True))
           - jnp.exp(jnp.sum(lq2_ref[...] * lk2_ref[...], axis=-1, keepdims=True)) + lam_init)
    for h in range(n_maps // 2):
        a0 = acc_scr[2 * h, 0:DIFF_V, :] / acc_scr[2 * h, DIFF_V:DIFF_V + 1, :]
        a1 = acc_scr[2 * h + 1, 0:DIFF_V, :] / acc_scr[2 * h + 1, DIFF_V:DIFF_V + 1, :]
        o_t = a0 - lam * a1
        o = o_t.T
        o_ref[:, h * DIFF_V:(h + 1) * DIFF_V] = (_rms(o, gsub_ref[...]) * (1.0 - lam_init)).astype(BF16)


def _oproj_body(n_x_tiles, om_ref, od_ref, wo_ref, x_ref, meta_ref, g_post_ref, g_pre_ref, h1_ref, n2_ref):
    m = pl.program_id(0)
    n_m = om_ref.shape[1]
    a = _dot(om_ref[...], wo_ref[0:n_m, :]) + _dot(od_ref[...], wo_ref[n_m:, :])
    h1_ref[...] = _rms(a, g_post_ref[...])

    @pl.when(m < n_x_tiles)
    def _():
        h1_ref[...] += x_ref[...]

    @pl.when(m == n_x_tiles)
    def _():
        h1_ref[...] += meta_ref[...]

    n2_ref[...] = _rms(h1_ref[...], g_pre_ref[...]).astype(BF16)


def _ffn_body(n_f,
              n2_ref, halo_ref, h1_ref, wg_ref, wv_ref, wd_ref, cwg_ref, cwv_ref, cbg_ref, cbv_ref, g_ref,
              out_ref, n2e_scr, upg_scr, upv_scr):
    f = pl.program_id(1)
    tm = n2_ref.shape[0]

    @pl.when(f == 0)
    def _():
        n2e_scr[0:FFN_HALO, :] = halo_ref[...]
        n2e_scr[FFN_HALO:, :] = n2_ref[...]
        out_ref[...] = jnp.zeros(out_ref.shape, F32)

    upg_scr[...] = _dot(n2e_scr[...], wg_ref[...])
    upv_scr[...] = _dot(n2e_scr[...], wv_ref[...])

    def conv(up_scr, cw_ref, cb_ref):
        acc = cb_ref[...] + cw_ref[0:1, :] * up_scr[FFN_HALO - 2:FFN_HALO - 2 + tm, :]
        acc = acc + cw_ref[1:2, :] * up_scr[FFN_HALO - 1:FFN_HALO - 1 + tm, :]
        return acc + cw_ref[2:3, :] * up_scr[FFN_HALO:FFN_HALO + tm, :]

    gate = conv(upg_scr, cwg_ref, cbg_ref)
    val = conv(upv_scr, cwv_ref, cbv_ref)
    act = (gate / (1.0 + jnp.exp(-gate)) * val).astype(BF16)
    out_ref[...] += _dot(act, wd_ref[...])

    @pl.when(f == n_f - 1)
    def _():
        out_ref[...] = h1_ref[...] + _rms(out_ref[...], g_ref[...])


def _t5_bucket_np(n):
    n = np.maximum(n, 0)
    max_exact = NUM_BUCKETS // 2
    nf = np.maximum(n, 1).astype(np.float32)
    large = max_exact + (np.log(nf / np.float32(max_exact)) / np.float32(math.log(REL_MAX_DIST / max_exact))
                         * np.float32(NUM_BUCKETS - max_exact)).astype(np.int32)
    large = np.minimum(large, NUM_BUCKETS - 1)
    return np.where(n < max_exact, n, large).astype(np.int32)


def _bias_patterns():
    kk = np.arange(TILE)[:, None]
    qq = np.arange(TILE)[None, :]
    return np.stack([_t5_bucket_np(qq - kk), _t5_bucket_np(TILE + qq - kk), _t5_bucket_np(qq + N_META - kk)])


def kernel(x, meta_tokens, rel_bias, w_in, w_uq, w_ukv, w_o, g_attn_pre, g_attn_post, g_cq, g_ckv,
           lambda_q1, lambda_k1, lambda_q2, lambda_k2, g_diff_sub, g_ffn_pre, g_ffn_post,
           w_up, conv_w, conv_b, w_down):
    B, S, D = x.shape
    assert w_in.shape[0] == 1, "single-layer problem"
    assert S % QTILE == 0 and S % FFN_TM == 0
    d_ff = w_down.shape[1]
    assert d_ff % FFN_TF == 0
    tpb = S // TILE
    nxt = B * tpb
    n_meta_tiles = QTILE // TILE
    nt = nxt + n_meta_tiles
    R = nt * TILE
    qpb = S // QTILE
    nxq = B * qpb
    n_bias_maps = rel_bias.shape[1]
    lam_init = 0.8 - 0.6 * math.exp(-0.3 * 0)
    d_mix = MLA_HEADS * MLA_V + DIFF_HEADS * DIFF_V
    mv_rows = MLA_V + ONES_ROWS
    dv_rows = DIFF_V + ONES_ROWS

    def rot_cols(w):
        half = w.shape[-1] // 2
        return jnp.concatenate([-w[..., half:], w[..., :half]], axis=-1)

    wi = w_in[0].astype(BF16)
    c0 = Q_LORA + KV_LORA
    k_r = wi[:, c0:c0 + MLA_ROPE]
    w_in_b = jnp.concatenate([k_r, rot_cols(k_r), wi[:, c0 + MLA_ROPE:]], axis=1)
    wq = w_uq[0].reshape(Q_LORA, MLA_HEADS, MLA_QK)
    q_rope_w = wq[:, :, MLA_NOPE:]
    w_uq_p = jnp.concatenate([wq[:, :, :MLA_NOPE].reshape(Q_LORA, -1), q_rope_w.reshape(Q_LORA, -1),
                              rot_cols(q_rope_w).reshape(Q_LORA, -1)], axis=1).astype(BF16)
    wkv = w_ukv[0].reshape(KV_LORA, MLA_HEADS, MLA_NOPE + MLA_V)
    w_ukv_p = jnp.concatenate([wkv[:, :, :MLA_NOPE].reshape(KV_LORA, -1),
                               wkv[:, :, MLA_NOPE:].reshape(KV_LORA, -1)], axis=1).astype(BF16)
    n_slabs = 32
    assert nt >= n_slabs and all(w.shape[1] % (16 * n_slabs) == 0 for w in (w_o, w_up, w_down))
    slab_rows = lambda w: w.shape[1] // n_slabs

    half = MLA_ROPE // 2
    inv = ROPE_THETA ** (-jnp.arange(half, dtype=F32) / half)
    pos = jnp.concatenate([jnp.arange(S, dtype=jnp.int32) + N_META, jnp.arange(QTILE, dtype=jnp.int32)])
    ang = pos.astype(F32)[:, None] * inv[None, :]
    cos64 = jnp.tile(jnp.cos(ang), (1, 2))
    sin64 = jnp.tile(jnp.sin(ang), (1, 2))
    cos_q = jnp.tile(cos64, (1, 2))
    sin_q = jnp.tile(sin64, (1, 2))
    cs_k = jnp.concatenate([cos64, sin64], axis=1)

    x2 = x.reshape(B * S, D)
    meta_pad = jnp.concatenate([meta_tokens.astype(x.dtype), jnp.zeros((QTILE - N_META, D), x.dtype)], axis=0)
    row1 = lambda v: v.reshape(1, -1)

    x_idx = lambda m: (jnp.minimum(m, nxt - 1), 0)
    meta_idx = lambda m: (jnp.maximum(m - nxt, 0), 0)
    tab_idx = lambda m: (jnp.where(m < nxt, lax.rem(m, tpb), tpb + m - nxt), 0)
    const2 = lambda m: (0, 0)
    slab1 = lambda m: (jnp.minimum(m, n_slabs - 1), 0)
    d_in_p = c0 + w_in_b.shape[1]
    resident = dict(pipeline_mode=pl.Buffered(1))

    qm, km, vmt, qd, kd, vdt, w_o_b = pl.pallas_call(
        functools.partial(_proj_body, nxt, LOG2E / math.sqrt(MLA_QK), LOG2E / math.sqrt(DIFF_DK)),
        grid=(nt,),
        in_specs=[
            pl.BlockSpec((TILE, D), x_idx),
            pl.BlockSpec((TILE, D), meta_idx),
            pl.BlockSpec((TILE, cos_q.shape[1]), tab_idx),
            pl.BlockSpec((TILE, sin_q.shape[1]), tab_idx),
            pl.BlockSpec((TILE, 2 * MLA_ROPE), tab_idx),
            pl.BlockSpec((1, D), const2),
            pl.BlockSpec((D, c0), const2, **resident),
            pl.BlockSpec(w_in_b.shape, const2, **resident),
            pl.BlockSpec((1, Q_LORA), const2),
            pl.BlockSpec((1, KV_LORA), const2),
            pl.BlockSpec(w_uq_p.shape, const2, **resident),
            pl.BlockSpec(w_ukv_p.shape, const2, **resident),
            pl.BlockSpec((slab_rows(w_o), D), slab1),
        ],
        out_specs=[
            pl.BlockSpec((MLA_HEADS, TILE, MLA_QK), lambda m: (0, m, 0)),
            pl.BlockSpec((MLA_HEADS, 1, TILE, MLA_QK), lambda m: (0, m, 0, 0)),
            pl.BlockSpec((MLA_HEADS, 1, mv_rows, TILE), lambda m: (0, m, 0, 0)),
            pl.BlockSpec((2 * DIFF_HEADS, TILE, DIFF_DK), lambda m: (0, m, 0)),
            pl.BlockSpec((2 * DIFF_HEADS, 1, TILE, DIFF_DK), lambda m: (0, m, 0, 0)),
            pl.BlockSpec((DIFF_HEADS, 1, dv_rows, TILE), lambda m: (0, m, 0, 0)),
            pl.BlockSpec((slab_rows(w_o), D), slab1),
        ],
        out_shape=[
            jax.ShapeDtypeStruct((MLA_HEADS, R, MLA_QK), BF16),
            jax.ShapeDtypeStruct((MLA_HEADS, nt, TILE, MLA_QK), BF16),
            jax.ShapeDtypeStruct((MLA_HEADS, nt, mv_rows, TILE), BF16),
            jax.ShapeDtypeStruct((2 * DIFF_HEADS, R, DIFF_DK), BF16),
            jax.ShapeDtypeStruct((2 * DIFF_HEADS, nt, TILE, DIFF_DK), BF16),
            jax.ShapeDtypeStruct((DIFF_HEADS, nt, dv_rows, TILE), BF16),
            jax.ShapeDtypeStruct(w_o.shape[1:], BF16),
        ],
        scratch_shapes=[
            pltpu.VMEM((TILE, D), BF16),
            pltpu.VMEM((TILE, d_in_p), F32),
            pltpu.VMEM((TILE, w_uq_p.shape[1]), F32),
            pltpu.VMEM((TILE, w_ukv_p.shape[1]), F32),
        ],
        compiler_params=pltpu.CompilerParams(dimension_semantics=("arbitrary",), vmem_limit_bytes=VMEM_LIMIT),
        name="in_proj",
    )(x2, meta_pad, cos_q, sin_q, cs_k, row1(g_attn_pre[0]), wi, w_in_b, row1(g_cq[0]), row1(g_ckv[0]),
      w_uq_p, w_ukv_p, w_o[0])

    batch_of = lambda s: jnp.minimum(s // qpb, B - 1)

    gm = MLA_GROUP
    assert (MLA_HEADS // gm) * (nxq + 1) >= n_slabs and (DIFF_HEADS // DIFF_GROUP) * (nxq + 1) >= n_slabs
    slab2 = lambda h, s: (jnp.minimum(h * (nxq + 1) + s, n_slabs - 1), 0)
    o_m, w_up_b = pl.pallas_call(
        functools.partial(_mla_body, nxq, qpb),
        grid=(MLA_HEADS // gm, nxq + 1),
        in_specs=[
            pl.BlockSpec((gm, QTILE, MLA_QK), lambda h, s: (h, s, 0)),
            pl.BlockSpec((gm, tpb, TILE, MLA_QK), lambda h, s: (h, batch_of(s), 0, 0)),
            pl.BlockSpec((gm, tpb, mv_rows, TILE), lambda h, s: (h, batch_of(s), 0, 0)),
            pl.BlockSpec((gm, None, TILE, MLA_QK), lambda h, s: (h, nxt, 0, 0)),
            pl.BlockSpec((gm, None, mv_rows, TILE), lambda h, s: (h, nxt, 0, 0)),
            pl.BlockSpec((slab_rows(w_up), w_up.shape[2]), slab2),
        ],
        out_specs=[
            pl.BlockSpec((QTILE, gm * MLA_V), lambda h, s: (s, h)),
            pl.BlockSpec((slab_rows(w_up), w_up.shape[2]), slab2),
        ],
        out_shape=[
            jax.ShapeDtypeStruct((R, MLA_HEADS * MLA_V), BF16),
            jax.ShapeDtypeStruct(w_up.shape[1:], BF16),
        ],
        scratch_shapes=[
            pltpu.VMEM((gm, 1, QTILE), F32),
            pltpu.VMEM((gm, mv_rows, QTILE), F32),
            pltpu.VMEM((gm, TILE, QTILE), F32),
            pltpu.VMEM((gm, 1, QTILE), F32),
            pltpu.VMEM((gm, TILE, QTILE), F32),
            pltpu.VMEM((gm, 1, QTILE), F32),
        ],
        compiler_params=pltpu.CompilerParams(dimension_semantics=("arbitrary", "arbitrary"),
                                             vmem_limit_bytes=VMEM_LIMIT),
        name="mla_attn",
    )(qm, km, vmt, km, vmt, w_up[0])

    patterns = jnp.asarray(_bias_patterns())
    bias_tiles = pl.pallas_call(
        _bias_body,
        grid=(patterns.shape[0],),
        in_specs=[
            pl.BlockSpec((1, TILE, TILE), lambda p: (p, 0, 0)),
            pl.BlockSpec(memory_space=pltpu.SMEM),
        ],
        out_specs=pl.BlockSpec((1, n_bias_maps, TILE, TILE), lambda p: (p, 0, 0, 0)),
        out_shape=jax.ShapeDtypeStruct((patterns.shape[0], n_bias_maps, TILE, TILE), F32),
        compiler_params=pltpu.CompilerParams(dimension_semantics=("arbitrary",)),
        name="rel_bias_tiles",
    )(patterns, rel_bias.astype(F32))

    gd = DIFF_GROUP
    o_d, w_down_b = pl.pallas_call(
        functools.partial(_diff_body, nxq, qpb, lam_init),
        grid=(DIFF_HEADS // gd, nxq + 1),
        in_specs=[
            pl.BlockSpec((2 * gd, QTILE, DIFF_DK), lambda h, s: (h, s, 0)),
            pl.BlockSpec((2 * gd, tpb, TILE, DIFF_DK), lambda h, s: (h, batch_of(s), 0, 0)),
            pl.BlockSpec((gd, tpb, dv_rows, TILE), lambda h, s: (h, batch_of(s), 0, 0)),
            pl.BlockSpec((2 * gd, None, TILE, DIFF_DK), lambda h, s: (h, nxt, 0, 0)),
            pl.BlockSpec((gd, None, dv_rows, TILE), lambda h, s: (h, nxt, 0, 0)),
            pl.BlockSpec((patterns.shape[0], 2 * gd, TILE, TILE), lambda h, s: (0, h, 0, 0)),
            pl.BlockSpec((1, DIFF_DK), lambda h, s: (0, 0)),
            pl.BlockSpec((1, DIFF_DK), lambda h, s: (0, 0)),
            pl.BlockSpec((1, DIFF_DK), lambda h, s: (0, 0)),
            pl.BlockSpec((1, DIFF_DK), lambda h, s: (0, 0)),
            pl.BlockSpec((1, DIFF_V), lambda h, s: (0, 0)),
            pl.BlockSpec((slab_rows(w_down), D), slab2),
        ],
        out_specs=[
            pl.BlockSpec((QTILE, gd * DIFF_V), lambda h, s: (s, h)),
            pl.BlockSpec((slab_rows(w_down), D), slab2),
        ],
        out_shape=[
            jax.ShapeDtypeStruct((R, DIFF_HEADS * DIFF_V), BF16),
            jax.ShapeDtypeStruct(w_down.shape[1:], BF16),
        ],
        scratch_shapes=[
            pltpu.VMEM((2 * gd, 1, QTILE), F32),
            pltpu.VMEM((2 * gd, dv_rows, QTILE), F32),
            pltpu.VMEM((2 * gd, TILE, QTILE), F32),
            pltpu.VMEM((2 * gd, 1, QTILE), F32),
            pltpu.VMEM((2 * gd, TILE, QTILE), F32),
            pltpu.VMEM((2 * gd, 1, QTILE), F32),
        ],
        compiler_params=pltpu.CompilerParams(dimension_semantics=("arbitrary", "arbitrary"),
                                             vmem_limit_bytes=VMEM_LIMIT),
        name="diff_attn",
    )(qd, kd, vdt, kd, vdt, bias_tiles, lambda_q1.astype(F32), lambda_k1.astype(F32),
      lambda_q2.astype(F32), lambda_k2.astype(F32), row1(g_diff_sub[0]), w_down[0])

    nt_o = nxt + 1
    h1, n2 = pl.pallas_call(
        functools.partial(_oproj_body, nxt),
        grid=(nt_o,),
        in_specs=[
            pl.BlockSpec((TILE, MLA_HEADS * MLA_V), lambda m: (m, 0)),
            pl.BlockSpec((TILE, DIFF_HEADS * DIFF_V), lambda m: (m, 0)),
            pl.BlockSpec((d_mix, D), const2, **resident),
            pl.BlockSpec((TILE, D), x_idx),
            pl.BlockSpec((TILE, D), const2),
            pl.BlockSpec((1, D), const2),
            pl.BlockSpec((1, D), const2),
        ],
        out_specs=[
            pl.BlockSpec((TILE, D), lambda m: (m, 0)),
            pl.BlockSpec((TILE, D), lambda m: (m, 0)),
        ],
        out_shape=[
            jax.ShapeDtypeStruct((nt_o * TILE, D), F32),
            jax.ShapeDtypeStruct((nt_o * TILE, D), BF16),
        ],
        compiler_params=pltpu.CompilerParams(dimension_semantics=("arbitrary",), vmem_limit_bytes=VMEM_LIMIT),
        name="out_proj",
    )(o_m, o_d, w_o_b, x2, meta_pad, row1(g_attn_post[0]), row1(g_ffn_pre[0]))

    n_m = (B * S) // FFN_TM
    n_f = d_ff // FFN_TF
    tiles_per_batch_ffn = S // FFN_TM
    halo_per_tile = FFN_TM // FFN_HALO
    meta_halo_blk = (B * S) // FFN_HALO

    def halo_idx(m, f):
        return (jnp.where(lax.rem(m, tiles_per_batch_ffn) == 0, meta_halo_blk, m * halo_per_tile - 1), 0)

    out = pl.pallas_call(
        functools.partial(_ffn_body, n_f),
        grid=(n_m, n_f),
        in_specs=[
            pl.BlockSpec((FFN_TM, D), lambda m, f: (m, 0)),
            pl.BlockSpec((FFN_HALO, D), halo_idx),
            pl.BlockSpec((FFN_TM, D), lambda m, f: (m, 0)),
            pl.BlockSpec((D, FFN_TF), lambda m, f: (0, f)),
            pl.BlockSpec((D, FFN_TF), lambda m, f: (0, n_f + f)),
            pl.BlockSpec((FFN_TF, D), lambda m, f: (f, 0)),
            pl.BlockSpec((CONV_W, FFN_TF), lambda m, f: (0, f)),
            pl.BlockSpec((CONV_W, FFN_TF), lambda m, f: (0, n_f + f)),
            pl.BlockSpec((1, FFN_TF), lambda m, f: (0, f)),
            pl.BlockSpec((1, FFN_TF), lambda m, f: (0, n_f + f)),
            pl.BlockSpec((1, D), lambda m, f: (0, 0)),
        ],
        out_specs=pl.BlockSpec((FFN_TM, D), lambda m, f: (m, 0)),
        out_shape=jax.ShapeDtypeStruct((B * S, D), x.dtype),
        scratch_shapes=[
            pltpu.VMEM((FFN_HALO + FFN_TM, D), BF16),
            pltpu.VMEM((FFN_HALO + FFN_TM, FFN_TF), F32),
            pltpu.VMEM((FFN_HALO + FFN_TM, FFN_TF), F32),
        ],
        compiler_params=pltpu.CompilerParams(dimension_semantics=("arbitrary", "arbitrary"),
                                             vmem_limit_bytes=VMEM_LIMIT),
        name="conv_ffn",
    )(n2, n2, h1, w_up_b, w_up_b, w_down_b, conv_w[0], conv_w[0], row1(conv_b[0]), row1(conv_b[0]),
      row1(g_ffn_post[0]))

    return out.reshape(B, S, D)
```

```python
import functools
import math

import numpy as np
import jax
import jax.numpy as jnp
from jax import lax
from jax.experimental import pallas as pl
from jax.experimental.pallas import tpu as pltpu

F32 = jnp.float32
BF16 = jnp.bfloat16

N_META = 16
MLA_HEADS = 8
MLA_NOPE = 128
MLA_ROPE = 64
MLA_V = 128
MLA_QK = MLA_NOPE + MLA_ROPE
Q_LORA = 512
KV_LORA = 256
ROPE_THETA = 10000.0
DIFF_HEADS = 4
DIFF_DK = 128
DIFF_V = 2 * DIFF_DK
DIFF_QK = DIFF_HEADS * 2 * DIFF_DK
NUM_BUCKETS = 32
REL_MAX_DIST = 128
CONV_W = 3
EPS = 1e-6
LOG2E = math.log2(math.e)

TILE = 256
QTILE = 2 * TILE
META_KEYS = 128
ONES_ROWS = 16
MLA_GROUP = 4
DIFF_GROUP = 2
FFN_TM = 512
FFN_TF = 512
FFN_HALO = 16
VMEM_LIMIT = 56 * 1024 * 1024


def _rms(x, g):
    ms = jnp.mean(x * x, axis=-1, keepdims=True)
    return x * lax.rsqrt(ms + EPS) * g


def _dot(a, b):
    return jnp.dot(a, b, preferred_element_type=F32)


def _dot_nt(a, b):
    return lax.dot_general(a, b, (((1,), (1,)), ((), ())), preferred_element_type=F32)


def _proj_body(n_x_tiles, q_scale_mla, q_scale_diff,
               x_ref, meta_ref, cos_ref, sin_ref, cs_ref, g_pre_ref, w_in_ref, g_cq_ref, g_ckv_ref,
               w_uq_ref, w_ukv_ref, wcast_ref,
               qm_ref, km_ref, vmt_ref, qd_ref, kd_ref, vdt_ref, wcast_out_ref,
               n_scr, proj_scr, qm_scr, kv_scr):
    m = pl.program_id(0)
    wcast_out_ref[...] = wcast_ref[...].astype(BF16)

    @pl.when(m < n_x_tiles)
    def _():
        n_scr[...] = _rms(x_ref[...], g_pre_ref[...]).astype(BF16)

    @pl.when(m >= n_x_tiles)
    def _():
        n_scr[...] = _rms(meta_ref[...], g_pre_ref[...]).astype(BF16)

    n_lat = Q_LORA + KV_LORA
    proj_scr[...] = _dot_nt(n_scr[...], w_in_ref[...])
    c_q = _rms(proj_scr[:, 0:Q_LORA], g_cq_ref[...]).astype(BF16)
    c_kv = _rms(proj_scr[:, Q_LORA:n_lat], g_ckv_ref[...]).astype(BF16)
    qm_scr[...] = _dot(c_q, w_uq_ref[...])
    kv_scr[...] = _dot(c_kv, w_ukv_ref[...])

    n_nope = MLA_HEADS * MLA_NOPE
    n_rope = MLA_HEADS * MLA_ROPE
    lanes = cos_ref.shape[1]
    q_rope = jnp.concatenate(
        [qm_scr[:, n_nope + c:n_nope + c + lanes] * cos_ref[...]
         + qm_scr[:, n_nope + n_rope + c:n_nope + n_rope + c + lanes] * sin_ref[...]
         for c in range(0, n_rope, lanes)], axis=1)
    k_r = proj_scr[:, n_lat:n_lat + MLA_ROPE]
    k_rot = jnp.concatenate([-k_r[:, MLA_ROPE // 2:], k_r[:, 0:MLA_ROPE // 2]], axis=1)
    k_rope = (k_r * cs_ref[:, 0:MLA_ROPE] + k_rot * cs_ref[:, MLA_ROPE:2 * MLA_ROPE]).astype(BF16)

    for h in range(MLA_HEADS):
        qm_ref[h, :, 0:MLA_NOPE] = (qm_scr[:, h * MLA_NOPE:(h + 1) * MLA_NOPE] * q_scale_mla).astype(BF16)
        qm_ref[h, :, MLA_NOPE:MLA_QK] = (q_rope[:, h * MLA_ROPE:(h + 1) * MLA_ROPE] * q_scale_mla).astype(BF16)
        km_ref[h, 0, :, 0:MLA_NOPE] = kv_scr[:, h * MLA_NOPE:(h + 1) * MLA_NOPE].astype(BF16)
        km_ref[h, 0, :, MLA_NOPE:MLA_QK] = k_rope
        v = kv_scr[:, n_nope + h * MLA_V:n_nope + (h + 1) * MLA_V]
        vmt_ref[h, 0, 0:MLA_V, :] = v.T.astype(BF16)
        vmt_ref[h, 0, MLA_V:MLA_V + ONES_ROWS, :] = jnp.ones((ONES_ROWS, TILE), BF16)

    off_qd = n_lat + MLA_ROPE
    off_kd = off_qd + DIFF_QK
    off_vd = off_kd + DIFF_QK
    for i in range(2 * DIFF_HEADS):
        qd_ref[i] = (proj_scr[:, off_qd + i * DIFF_DK:off_qd + (i + 1) * DIFF_DK] * q_scale_diff).astype(BF16)
        kd_ref[i, 0] = proj_scr[:, off_kd + i * DIFF_DK:off_kd + (i + 1) * DIFF_DK].astype(BF16)
    for h in range(DIFF_HEADS):
        v = proj_scr[:, off_vd + h * DIFF_V:off_vd + (h + 1) * DIFF_V]
        vdt_ref[h, 0, 0:DIFF_V, :] = v.T.astype(BF16)
        vdt_ref[h, 0, DIFF_V:DIFF_V + ONES_ROWS, :] = jnp.ones((ONES_ROWS, TILE), BF16)


def _meta_valid(is_x):
    row = lax.broadcasted_iota(jnp.int32, (META_KEYS, QTILE), 0)
    col = lax.broadcasted_iota(jnp.int32, (META_KEYS, QTILE), 1)
    return row <= jnp.where(is_x, N_META - 1, jnp.minimum(col, N_META - 1))


def _diag_valid(kind):
    if kind not in ("diag_lo", "diag_hi"):
        return None
    row = lax.broadcasted_iota(jnp.int32, (TILE, QTILE), 0)
    col = lax.broadcasted_iota(jnp.int32, (TILE, QTILE), 1)
    return row - col <= (0 if kind == "diag_lo" else -TILE)


def _chunk_pipeline(is_x, u, qk_meta, qk_x, update_meta, update_x, b0, b1):
    qk_meta(b0)

    @pl.when(jnp.logical_not(is_x))
    def _():
        update_meta(b0)

    @pl.when(is_x)
    def _():
        qk_x(b1, 2 * u + 1, "diag_hi")
        update_meta(b0)
        qk_x(b0, 2 * u, "diag_lo")
        update_x(b1, 2 * u + 1)

        @pl.when(u == 0)
        def _():
            update_x(b0, 2 * u)

        @pl.when(u >= 1)
        def _():
            qk_x(b1, 2 * u - 1, "prev")
            update_x(b0, 2 * u)

            def pair(k, carry):
                qk_x(b0, 2 * k, "far")
                update_x(b1, jnp.where(k == 0, 2 * u - 1, 2 * k - 1))
                qk_x(b1, 2 * k + 1, "far")
                update_x(b0, 2 * k)
                return carry

            lax.fori_loop(0, u - 1, pair, 0)
            qk_x(b0, 2 * u - 2, "far")
            update_x(b1, jnp.where(u == 1, 2 * u - 1, 2 * u - 3))
            update_x(b0, 2 * u - 2)


def _mla_body(n_x_tiles, tiles_per_batch,
              q_ref, k_ref, vt_ref, kmeta_ref, vtmeta_ref, wcast_ref, o_ref, wcast_out_ref,
              m_scr, acc_scr, st0_scr, mx0_scr, st1_scr, mx1_scr):
    n_heads = q_ref.shape[0]
    s = pl.program_id(1)
    is_x = s < n_x_tiles
    u = lax.rem(s, tiles_per_batch)
    wcast_out_ref[...] = wcast_ref[...].astype(BF16)

    for g in range(n_heads):
        m_scr[g] = jnp.full((1, QTILE), -jnp.inf, F32)
        acc_scr[g] = jnp.zeros((MLA_V + ONES_ROWS, QTILE), F32)

    def qk_meta(buf):
        st_buf, mx_buf = buf
        valid = _meta_valid(is_x)
        for g in range(n_heads):
            st = jnp.where(valid, _dot_nt(kmeta_ref[g, 0:META_KEYS, :], q_ref[g]), -jnp.inf)
            st_buf[g, 0:META_KEYS, :] = st
            mx_buf[g] = jnp.max(st, axis=0, keepdims=True)

    def qk_x(buf, j, kind):
        st_buf, mx_buf = buf
        valid = _diag_valid(kind)
        for g in range(n_heads):
            st = _dot_nt(k_ref[g, j], q_ref[g])
            if valid is not None:
                st = jnp.where(valid, st, -jnp.inf)
            st_buf[g] = st
            mx_buf[g] = jnp.max(st, axis=0, keepdims=True)

    def update(buf, rows, vt_of):
        st_buf, mx_buf = buf
        for g in range(n_heads):
            m_prev = m_scr[g]
            m_new = jnp.maximum(m_prev, mx_buf[g])
            alpha = jnp.exp2(m_prev - m_new)
            p = jnp.exp2(st_buf[g, 0:rows, :] - m_new)
            acc_scr[g] = alpha * acc_scr[g] + _dot(vt_of(g), p.astype(BF16))
            m_scr[g] = m_new

    def update_meta(buf):
        update(buf, META_KEYS, lambda g: vtmeta_ref[g, :, 0:META_KEYS])

    def update_x(buf, j):
        update(buf, TILE, lambda g: vt_ref[g, j])

    _chunk_pipeline(is_x, u, qk_meta, qk_x, update_meta, update_x, (st0_scr, mx0_scr), (st1_scr, mx1_scr))

    for g in range(n_heads):
        o_t = acc_scr[g, 0:MLA_V, :] / acc_scr[g, MLA_V:MLA_V + 1, :]
        o_ref[:, g * MLA_V:(g + 1) * MLA_V] = o_t.T.astype(BF16)


def _bias_body(idx_ref, rel_ref, out_ref):
    idx = idx_ref[0]
    n_maps = out_ref.shape[1]
    for mp in range(n_maps):
        far = jnp.full(idx.shape, rel_ref[NUM_BUCKETS - 1, mp], F32)
        acc = far
        for b in range(NUM_BUCKETS - 1):
            acc = jnp.where(idx == b, rel_ref[b, mp], acc)
        out_ref[0, mp] = (acc - far) * LOG2E


def _diff_body(n_x_tiles, tiles_per_batch, lam_init,
               q_ref, k_ref, vt_ref, kmeta_ref, vtmeta_ref, bias_ref,
               lq1_ref, lk1_ref, lq2_ref, lk2_ref, gsub_ref, wcast_ref, o_ref, wcast_out_ref,
               m_scr, acc_scr, st0_scr, mx0_scr, st1_scr, mx1_scr):
    n_maps = q_ref.shape[0]
    s = pl.program_id(1)
    is_x = s < n_x_tiles
    u = lax.rem(s, tiles_per_batch)
    wcast_out_ref[...] = wcast_ref[...].astype(BF16)

    for i in range(n_maps):
        m_scr[i] = jnp.full((1, QTILE), -jnp.inf, F32)
        acc_scr[i] = jnp.zeros((DIFF_V + ONES_ROWS, QTILE), F32)

    def qk_meta(buf):
        st_buf, mx_buf = buf
        valid = _meta_valid(is_x)
        w_first = jnp.where(jnp.logical_and(is_x, u == 0), 1.0, 0.0).astype(F32)
        w_meta = jnp.where(is_x, 0.0, 1.0).astype(F32)
        for i in range(n_maps):
            b = w_first * bias_ref[2, i, 0:META_KEYS, :] + w_meta * bias_ref[0, i, 0:META_KEYS, :]
            st = _dot_nt(kmeta_ref[i, 0:META_KEYS, :], q_ref[i])
            st = jnp.concatenate([st[:, 0:TILE] + b, st[:, TILE:QTILE]], axis=1)
            st = jnp.where(valid, st, -jnp.inf)
            st_buf[i, 0:META_KEYS, :] = st
            mx_buf[i] = jnp.max(st, axis=0, keepdims=True)

    def qk_x(buf, j, kind):
        st_buf, mx_buf = buf
        valid = _diag_valid(kind)
        for i in range(n_maps):
            st = _dot_nt(k_ref[i, j], q_ref[i])
            if kind == "diag_lo":
                st = st + jnp.concatenate([bias_ref[0, i], bias_ref[1, i]], axis=1)
            elif kind == "diag_hi":
                st = st + jnp.concatenate([bias_ref[0, i], bias_ref[0, i]], axis=1)
            elif kind == "prev":
                st = jnp.concatenate([st[:, 0:TILE] + bias_ref[1, i], st[:, TILE:QTILE]], axis=1)
            if valid is not None:
                st = jnp.where(valid, st, -jnp.inf)
            st_buf[i] = st
            mx_buf[i] = jnp.max(st, axis=0, keepdims=True)

    def update(buf, rows, vt_of):
        st_buf, mx_buf = buf
        for i in range(n_maps):
            m_prev = m_scr[i]
            m_new = jnp.maximum(m_prev, mx_buf[i])
            alpha = jnp.exp2(m_prev - m_new)
            p = jnp.exp2(st_buf[i, 0:rows, :] - m_new)
            acc_scr[i] = alpha * acc_scr[i] + _dot(vt_of(i // 2), p.astype(BF16))
            m_scr[i] = m_new

    def update_meta(buf):
        update(buf, META_KEYS, lambda h: vtmeta_ref[h, :, 0:META_KEYS])

    def update_x(buf, j):
        update(buf, TILE, lambda h: vt_ref[h, j])

    _chunk_pipeline(is_x, u, qk_meta, qk_x, update_meta, update_x, (st0_scr, mx0_scr), (st1_scr, mx1_scr))

    lam =(jnp.exp(jnp.sum(lq1_ref[...] * lk1_ref[...], axis=-1, keepdims=True))
           - jnp.exp(jnp.sum(lq2_ref[...] * lk2_ref[...], axis=-1, keepdims=True)) + lam_init)
    for h in range(n_maps // 2):
        a0 = acc_scr[2 * h, 0:DIFF_V, :] / acc_scr[2 * h, DIFF_V:DIFF_V + 1, :]
        a1 = acc_scr[2 * h + 1, 0:DIFF_V, :] / acc_scr[2 * h + 1, DIFF_V:DIFF_V + 1, :]
        o_t = a0 - lam * a1
        o = o_t.T
        o_ref[:, h * DIFF_V:(h + 1) * DIFF_V] = (_rms(o, gsub_ref[...]) * (1.0 - lam_init)).astype(BF16)


def _oproj_body(n_x_tiles, om_ref, od_ref, wo_ref, x_ref, meta_ref, g_post_ref, g_pre_ref, h1_ref, n2_ref):
    m = pl.program_id(0)
    n_m = om_ref.shape[1]
    a = _dot(om_ref[...], wo_ref[0:n_m, :]) + _dot(od_ref[...], wo_ref[n_m:, :])
    h1_ref[...] = _rms(a, g_post_ref[...])

    @pl.when(m < n_x_tiles)
    def _():
        h1_ref[...] += x_ref[...]

    @pl.when(m == n_x_tiles)
    def _():
        h1_ref[...] += meta_ref[...]

    n2_ref[...] = _rms(h1_ref[...], g_pre_ref[...]).astype(BF16)


def _ffn_body(n_f,
              n2_ref, halo_ref, h1_ref, wg_ref, wv_ref, wd_ref, cwg_ref, cwv_ref, cbg_ref, cbv_ref, g_ref,
              out_ref, n2e_scr, upg_scr, upv_scr):
    f = pl.program_id(1)
    tm = n2_ref.shape[0]

    @pl.when(f == 0)
    def _():
        n2e_scr[0:FFN_HALO, :] = halo_ref[...]
        n2e_scr[FFN_HALO:, :] = n2_ref[...]
        out_ref[...] = jnp.zeros(out_ref.shape, F32)

    upg_scr[...] = _dot(n2e_scr[...], wg_ref[...])
    upv_scr[...] = _dot(n2e_scr[...], wv_ref[...])

    def conv(up_scr, cw_ref, cb_ref):
        acc = cb_ref[...] + cw_ref[0:1, :] * up_scr[FFN_HALO - 2:FFN_HALO - 2 + tm, :]
        acc = acc + cw_ref[1:2, :] * up_scr[FFN_HALO - 1:FFN_HALO - 1 + tm, :]
        return acc + cw_ref[2:3, :] * up_scr[FFN_HALO:FFN_HALO + tm, :]

    gate = conv(upg_scr, cwg_ref, cbg_ref)
    val = conv(upv_scr, cwv_ref, cbv_ref)
    act = (gate / (1.0 + jnp.exp(-gate)) * val).astype(BF16)
    out_ref[...] += _dot(act, wd_ref[...])

    @pl.when(f == n_f - 1)
    def _():
        out_ref[...] = h1_ref[...] + _rms(out_ref[...], g_ref[...])


def _t5_bucket_np(n):
    n = np.maximum(n, 0)
    max_exact = NUM_BUCKETS // 2
    nf = np.maximum(n, 1).astype(np.float32)
    large = max_exact + (np.log(nf / np.float32(max_exact)) / np.float32(math.log(REL_MAX_DIST / max_exact))
                         * np.float32(NUM_BUCKETS - max_exact)).astype(np.int32)
    large = np.minimum(large, NUM_BUCKETS - 1)
    return np.where(n < max_exact, n, large).astype(np.int32)


def _bias_patterns():
    kk = np.arange(TILE)[:, None]
    qq = np.arange(TILE)[None, :]
    return np.stack([_t5_bucket_np(qq - kk), _t5_bucket_np(TILE + qq - kk), _t5_bucket_np(qq + N_META - kk)])


def kernel(x, meta_tokens, rel_bias, w_in, w_uq, w_ukv, w_o, g_attn_pre, g_attn_post, g_cq, g_ckv,
           lambda_q1, lambda_k1, lambda_q2, lambda_k2, g_diff_sub, g_ffn_pre, g_ffn_post,
           w_up, conv_w, conv_b, w_down):
    B, S, D = x.shape
    assert w_in.shape[0] == 1, "single-layer problem"
    assert S % QTILE == 0 and S % FFN_TM == 0
    d_ff = w_down.shape[1]
    assert d_ff % FFN_TF == 0
    tpb = S // TILE
    nxt = B * tpb
    n_meta_tiles = QTILE // TILE
    nt = nxt + n_meta_tiles
    R = nt * TILE
    qpb = S // QTILE
    nxq = B * qpb
    n_bias_maps = rel_bias.shape[1]
    lam_init = 0.8 - 0.6 * math.exp(-0.3 * 0)
    d_mix = MLA_HEADS * MLA_V + DIFF_HEADS * DIFF_V
    mv_rows = MLA_V + ONES_ROWS
    dv_rows = DIFF_V + ONES_ROWS

    def rot_cols(w):
        half = w.shape[-1] // 2
        return jnp.concatenate([-w[..., half:], w[..., :half]], axis=-1)

    wi = jnp.swapaxes(w_in[0], 0, 1).astype(BF16)
    wq = w_uq[0].reshape(Q_LORA, MLA_HEADS, MLA_QK)
    q_rope_w = wq[:, :, MLA_NOPE:]
    w_uq_p = jnp.concatenate([wq[:, :, :MLA_NOPE].reshape(Q_LORA, -1), q_rope_w.reshape(Q_LORA, -1),
                              rot_cols(q_rope_w).reshape(Q_LORA, -1)], axis=1).astype(BF16)
    wkv = w_ukv[0].reshape(KV_LORA, MLA_HEADS, MLA_NOPE + MLA_V)
    w_ukv_p = jnp.concatenate([wkv[:, :, :MLA_NOPE].reshape(KV_LORA, -1),
                               wkv[:, :, MLA_NOPE:].reshape(KV_LORA, -1)], axis=1).astype(BF16)
    n_slabs = 32
    assert nt >= n_slabs and all(w.shape[1] % (16 * n_slabs) == 0 for w in (w_o, w_up, w_down))
    slab_rows = lambda w: w.shape[1] // n_slabs

    half = MLA_ROPE // 2
    inv = ROPE_THETA ** (-jnp.arange(half, dtype=F32) / half)
    pos = jnp.concatenate([jnp.arange(S, dtype=jnp.int32) + N_META, jnp.arange(QTILE, dtype=jnp.int32)])
    ang = pos.astype(F32)[:, None] * inv[None, :]
    cos64 = jnp.tile(jnp.cos(ang), (1, 2))
    sin64 = jnp.tile(jnp.sin(ang), (1, 2))
    cos_q = jnp.tile(cos64, (1, 2))
    sin_q = jnp.tile(sin64, (1, 2))
    cs_k = jnp.concatenate([cos64, sin64], axis=1)

    x2 = x.reshape(B * S, D)
    meta_pad = jnp.concatenate([meta_tokens.astype(x.dtype), jnp.zeros((QTILE - N_META, D), x.dtype)], axis=0)
    row1 = lambda v: v.reshape(1, -1)

    x_idx = lambda m: (jnp.minimum(m, nxt - 1), 0)
    meta_idx = lambda m: (jnp.maximum(m - nxt, 0), 0)
    tab_idx = lambda m: (jnp.where(m < nxt, lax.rem(m, tpb), tpb + m - nxt), 0)
    const2 = lambda m: (0, 0)
    slab1 = lambda m: (jnp.minimum(m, n_slabs - 1), 0)
    d_in_p = wi.shape[0]
    resident = dict(pipeline_mode=pl.Buffered(1))

    qm, km, vmt, qd, kd, vdt, w_o_b = pl.pallas_call(
        functools.partial(_proj_body, nxt, LOG2E / math.sqrt(MLA_QK), LOG2E / math.sqrt(DIFF_DK)),
        grid=(nt,),
        in_specs=[
            pl.BlockSpec((TILE, D), x_idx),
            pl.BlockSpec((TILE, D), meta_idx),
            pl.BlockSpec((TILE, cos_q.shape[1]), tab_idx),
            pl.BlockSpec((TILE, sin_q.shape[1]), tab_idx),
            pl.BlockSpec((TILE, 2 * MLA_ROPE), tab_idx),
            pl.BlockSpec((1, D), const2),
            pl.BlockSpec((d_in_p, D), const2, **resident),
            pl.BlockSpec((1, Q_LORA), const2),
            pl.BlockSpec((1, KV_LORA), const2),
            pl.BlockSpec(w_uq_p.shape, const2, **resident),
            pl.BlockSpec(w_ukv_p.shape, const2, **resident),
            pl.BlockSpec((slab_rows(w_o), D), slab1),
        ],
        out_specs=[
            pl.BlockSpec((MLA_HEADS, TILE, MLA_QK), lambda m: (0, m, 0)),
            pl.BlockSpec((MLA_HEADS, 1, TILE, MLA_QK), lambda m: (0, m, 0, 0)),
            pl.BlockSpec((MLA_HEADS, 1, mv_rows, TILE), lambda m: (0, m, 0, 0)),
            pl.BlockSpec((2 * DIFF_HEADS, TILE, DIFF_DK), lambda m: (0, m, 0)),
            pl.BlockSpec((2 * DIFF_HEADS, 1, TILE, DIFF_DK), lambda m: (0, m, 0, 0)),
            pl.BlockSpec((DIFF_HEADS, 1, dv_rows, TILE), lambda m: (0, m, 0, 0)),
            pl.BlockSpec((slab_rows(w_o), D), slab1),
        ],
        out_shape=[
            jax.ShapeDtypeStruct((MLA_HEADS, R, MLA_QK), BF16),
            jax.ShapeDtypeStruct((MLA_HEADS, nt, TILE, MLA_QK), BF16),
            jax.ShapeDtypeStruct((MLA_HEADS, nt, mv_rows, TILE), BF16),
            jax.ShapeDtypeStruct((2 * DIFF_HEADS, R, DIFF_DK), BF16),
            jax.ShapeDtypeStruct((2 * DIFF_HEADS, nt, TILE, DIFF_DK), BF16),
            jax.ShapeDtypeStruct((DIFF_HEADS, nt, dv_rows, TILE), BF16),
            jax.ShapeDtypeStruct(w_o.shape[1:], BF16),
        ],
        scratch_shapes=[
            pltpu.VMEM((TILE, D), BF16),
            pltpu.VMEM((TILE, d_in_p), F32),
            pltpu.VMEM((TILE, w_uq_p.shape[1]), F32),
            pltpu.VMEM((TILE, w_ukv_p.shape[1]), F32),
        ],
        compiler_params=pltpu.CompilerParams(dimension_semantics=("arbitrary",), vmem_limit_bytes=VMEM_LIMIT),
        name="in_proj",
    )(x2, meta_pad, cos_q, sin_q, cs_k, row1(g_attn_pre[0]), wi, row1(g_cq[0]), row1(g_ckv[0]),
      w_uq_p, w_ukv_p, w_o[0])

    batch_of = lambda s: jnp.minimum(s // qpb, B - 1)

    gm = MLA_GROUP
    assert (MLA_HEADS // gm) * (nxq + 1) >= n_slabs and (DIFF_HEADS // DIFF_GROUP) * (nxq + 1) >= n_slabs
    slab2 = lambda h, s: (jnp.minimum(h * (nxq + 1) + s, n_slabs - 1), 0)
    o_m, w_up_b = pl.pallas_call(
        functools.partial(_mla_body, nxq, qpb),
        grid=(MLA_HEADS // gm, nxq + 1),
        in_specs=[
            pl.BlockSpec((gm, QTILE, MLA_QK), lambda h, s: (h, s, 0)),
            pl.BlockSpec((gm, tpb, TILE, MLA_QK), lambda h, s: (h, batch_of(s), 0, 0)),
            pl.BlockSpec((gm, tpb, mv_rows, TILE), lambda h, s: (h, batch_of(s), 0, 0)),
            pl.BlockSpec((gm, None, TILE, MLA_QK), lambda h, s: (h, nxt, 0, 0)),
            pl.BlockSpec((gm, None, mv_rows, TILE), lambda h, s: (h, nxt, 0, 0)),
            pl.BlockSpec((slab_rows(w_up), w_up.shape[2]), slab2),
        ],
        out_specs=[
            pl.BlockSpec((QTILE, gm * MLA_V), lambda h, s: (s, h)),
            pl.BlockSpec((slab_rows(w_up), w_up.shape[2]), slab2),
        ],
        out_shape=[
            jax.ShapeDtypeStruct((R, MLA_HEADS * MLA_V), BF16),
            jax.ShapeDtypeStruct(w_up.shape[1:], BF16),
        ],
        scratch_shapes=[
            pltpu.VMEM((gm, 1, QTILE), F32),
            pltpu.VMEM((gm, mv_rows, QTILE), F32),
            pltpu.VMEM((gm, TILE, QTILE), F32),
            pltpu.VMEM((gm, 1, QTILE), F32),
            pltpu.VMEM((gm, TILE, QTILE), F32),
            pltpu.VMEM((gm, 1, QTILE), F32),
        ],
        compiler_params=pltpu.CompilerParams(dimension_semantics=("arbitrary", "arbitrary"),
                                             vmem_limit_bytes=VMEM_LIMIT),
        name="mla_attn",
    )(qm, km, vmt, km, vmt, w_up[0])

    patterns = jnp.asarray(_bias_patterns())
    bias_tiles = pl.pallas_call(
        _bias_body,
        grid=(patterns.shape[0],),
        in_specs=[
            pl.BlockSpec((1, TILE, TILE), lambda p: (p, 0, 0)),
            pl.BlockSpec(memory_space=pltpu.SMEM),
        ],
        out_specs=pl.BlockSpec((1, n_bias_maps, TILE, TILE), lambda p: (p, 0, 0, 0)),
        out_shape=jax.ShapeDtypeStruct((patterns.shape[0], n_bias_maps, TILE, TILE), F32),
        compiler_params=pltpu.CompilerParams(dimension_semantics=("arbitrary",)),
        name="rel_bias_tiles",
    )(patterns, rel_bias.astype(F32))

    gd = DIFF_GROUP
    o_d, w_down_b = pl.pallas_call(
        functools.partial(_diff_body, nxq, qpb, lam_init),
        grid=(DIFF_HEADS // gd, nxq + 1),
        in_specs=[
            pl.BlockSpec((2 * gd, QTILE, DIFF_DK), lambda h, s: (h, s, 0)),
            pl.BlockSpec((2 * gd, tpb, TILE, DIFF_DK), lambda h, s: (h, batch_of(s), 0, 0)),
            pl.BlockSpec((gd, tpb, dv_rows, TILE), lambda h, s: (h, batch_of(s), 0, 0)),
            pl.BlockSpec((2 * gd, None, TILE, DIFF_DK), lambda h, s: (h, nxt, 0, 0)),
            pl.BlockSpec((gd, None, dv_rows, TILE), lambda h, s: (h, nxt, 0, 0)),
            pl.BlockSpec((patterns.shape[0], 2 * gd, TILE, TILE), lambda h, s: (0, h, 0, 0)),
            pl.BlockSpec((1, DIFF_DK), lambda h, s: (0, 0)),
            pl.BlockSpec((1, DIFF_DK), lambda h, s: (0, 0)),
            pl.BlockSpec((1, DIFF_DK), lambda h, s: (0, 0)),
            pl.BlockSpec((1, DIFF_DK), lambda h, s: (0, 0)),
            pl.BlockSpec((1, DIFF_V), lambda h, s: (0, 0)),
            pl.BlockSpec((slab_rows(w_down), D), slab2),
        ],
        out_specs=[
            pl.BlockSpec((QTILE, gd * DIFF_V), lambda h, s: (s, h)),
            pl.BlockSpec((slab_rows(w_down), D), slab2),
        ],
        out_shape=[
            jax.ShapeDtypeStruct((R, DIFF_HEADS * DIFF_V), BF16),
            jax.ShapeDtypeStruct(w_down.shape[1:], BF16),
        ],
        scratch_shapes=[
            pltpu.VMEM((2 * gd, 1, QTILE), F32),
            pltpu.VMEM((2 * gd, dv_rows, QTILE), F32),
            pltpu.VMEM((2 * gd, TILE, QTILE), F32),
            pltpu.VMEM((2 * gd, 1, QTILE), F32),
            pltpu.VMEM((2 * gd, TILE, QTILE), F32),
            pltpu.VMEM((2 * gd, 1, QTILE), F32),
        ],
        compiler_params=pltpu.CompilerParams(dimension_semantics=("arbitrary", "arbitrary"),
                                             vmem_limit_bytes=VMEM_LIMIT),
        name="diff_attn",
    )(qd, kd, vdt, kd, vdt, bias_tiles, lambda_q1.astype(F32), lambda_k1.astype(F32),
      lambda_q2.astype(F32), lambda_k2.astype(F32), row1(g_diff_sub[0]), w_down[0])

    nt_o = nxt + 1
    h1, n2 = pl.pallas_call(
        functools.partial(_oproj_body, nxt),
        grid=(nt_o,),
        in_specs=[
            pl.BlockSpec((TILE, MLA_HEADS * MLA_V), lambda m: (m, 0)),
            pl.BlockSpec((TILE, DIFF_HEADS * DIFF_V), lambda m: (m, 0)),
            pl.BlockSpec((d_mix, D), const2, **resident),
            pl.BlockSpec((TILE, D), x_idx),
            pl.BlockSpec((TILE, D), const2),
            pl.BlockSpec((1, D), const2),
            pl.BlockSpec((1, D), const2),
        ],
        out_specs=[
            pl.BlockSpec((TILE, D), lambda m: (m, 0)),
            pl.BlockSpec((TILE, D), lambda m: (m, 0)),
        ],
        out_shape=[
            jax.ShapeDtypeStruct((nt_o * TILE, D), F32),
            jax.ShapeDtypeStruct((nt_o * TILE, D), BF16),
        ],
        compiler_params=pltpu.CompilerParams(dimension_semantics=("arbitrary",), vmem_limit_bytes=VMEM_LIMIT),
        name="out_proj",
    )(o_m, o_d, w_o_b, x2, meta_pad, row1(g_attn_post[0]), row1(g_ffn_pre[0]))

    n_m = (B * S) // FFN_TM
    n_f = d_ff // FFN_TF
    tiles_per_batch_ffn = S // FFN_TM
    halo_per_tile = FFN_TM // FFN_HALO
    meta_halo_blk = (B * S) // FFN_HALO

    def halo_idx(m, f):
        return (jnp.where(lax.rem(m, tiles_per_batch_ffn) == 0, meta_halo_blk, m * halo_per_tile - 1), 0)

    out = pl.pallas_call(
        functools.partial(_ffn_body, n_f),
        grid=(n_m, n_f),
        in_specs=[
            pl.BlockSpec((FFN_TM, D), lambda m, f: (m, 0)),
            pl.BlockSpec((FFN_HALO, D), halo_idx),
            pl.BlockSpec((FFN_TM, D), lambda m, f: (m, 0)),
            pl.BlockSpec((D, FFN_TF), lambda m, f: (0, f)),
            pl.BlockSpec((D, FFN_TF), lambda m, f: (0, n_f + f)),
            pl.BlockSpec((FFN_TF, D), lambda m, f: (f, 0)),
            pl.BlockSpec((CONV_W, FFN_TF), lambda m, f: (0, f)),
            pl.BlockSpec((CONV_W, FFN_TF), lambda m, f: (0, n_f + f)),
            pl.BlockSpec((1, FFN_TF), lambda m, f: (0, f)),
            pl.BlockSpec((1, FFN_TF), lambda m, f: (0, n_f + f)),
            pl.BlockSpec((1, D), lambda m, f: (0, 0)),
        ],
        out_specs=pl.BlockSpec((FFN_TM, D), lambda m, f: (m, 0)),
        out_shape=jax.ShapeDtypeStruct((B * S, D), x.dtype),
        scratch_shapes=[
            pltpu.VMEM((FFN_HALO + FFN_TM, D), BF16),
            pltpu.VMEM((FFN_HALO + FFN_TM, FFN_TF), F32),
            pltpu.VMEM((FFN_HALO + FFN_TM, FFN_TF), F32),
        ],
        compiler_params=pltpu.CompilerParams(dimension_semantics=("arbitrary", "arbitrary"),
                                             vmem_limit_bytes=VMEM_LIMIT),
        name="conv_ffn",
    )(n2, n2, h1, w_up_b, w_up_b, w_down_b, conv_w[0], conv_w[0], row1(conv_b[0]), row1(conv_b[0]),
      row1(g_ffn_post[0]))

    return out.reshape(B, S, D)
```
